```python
import math
import jax, jax.numpy as jnp
from jax import lax
import numpy as np

D_MODEL = 1024
BATCH = 8
SEQ = 8192
DEPTH = 1

EPS = 1e-6
HEAD_DIM = 64
N_Q_HEADS = D_MODEL // 64
N_KV_HEADS = max(1, N_Q_HEADS // 8)
GROUP = N_Q_HEADS // N_KV_HEADS
WINDOW = 128
BLOCK = 128
RET_HEADS = D_MODEL // 256
RET_QK_DIM = 256
RET_V_DIM = 512
RET_CHUNK = 128
RET_ROT_BASE = 10000.0
D_FF = ((-(-8 * D_MODEL // 3) + 255) // 256) * 256

ATT_Q = N_Q_HEADS * HEAD_DIM
ATT_KV = N_KV_HEADS * HEAD_DIM
RET_QK = RET_HEADS * RET_QK_DIM
RET_V = RET_HEADS * RET_V_DIM
IN_SPLITS = (ATT_Q, ATT_KV, ATT_KV, RET_QK, RET_QK, RET_V, RET_V, D_MODEL, D_MODEL)
D_IN = sum(IN_SPLITS)

kernel_name = "hybrid_swa_sink_retention_gated_block"


def rmsnorm(x, gain):
    xf = x.astype(jnp.float32)
    xf = xf * lax.rsqrt(jnp.mean(xf * xf, axis=-1, keepdims=True) + EPS)
    return (xf * gain.astype(jnp.float32)).astype(x.dtype)


def rms_group_norm(x):
    xf = x.astype(jnp.float32)
    return xf * lax.rsqrt(jnp.mean(xf * xf, axis=-1, keepdims=True) + EPS)


def sliding_window_attention(q, k, v, q_gain, k_gain, sinks):
    b, s, _, _ = q.shape
    nb = s // BLOCK
    q = rmsnorm(q, q_gain)
    k = rmsnorm(k, k_gain)
    qb = q.reshape(b, nb, BLOCK, N_KV_HEADS, GROUP, HEAD_DIM)
    kb = k.reshape(b, nb, BLOCK, N_KV_HEADS, HEAD_DIM)
    vb = v.reshape(b, nb, BLOCK, N_KV_HEADS, HEAD_DIM)
    pad = ((0, 0), (1, 0), (0, 0), (0, 0), (0, 0))
    k_band = jnp.concatenate([jnp.pad(kb, pad)[:, :-1], kb], axis=2)
    v_band = jnp.concatenate([jnp.pad(vb, pad)[:, :-1], vb], axis=2)
    scores = jnp.einsum('bnqkgd,bnskd->bnkgqs', qb, k_band).astype(jnp.float32) * (HEAD_DIM ** -0.5)
    blk = jnp.arange(nb)[:, None, None]
    q_pos = blk * BLOCK + jnp.arange(BLOCK)[None, :, None]
    k_pos = blk * BLOCK - BLOCK + jnp.arange(2 * BLOCK)[None, None, :]
    allowed = (k_pos <= q_pos) & (k_pos > q_pos - WINDOW) & (k_pos >= 0)
    scores = jnp.where(allowed[None, :, None, None], scores, -jnp.inf)
    sink = sinks.astype(jnp.float32).reshape(N_KV_HEADS, GROUP)[None, None, :, :, None, None]
    sink = jnp.broadcast_to(sink, scores.shape[:-1] + (1,))
    probs = jax.nn.softmax(jnp.concatenate([scores, sink], axis=-1), axis=-1)[..., :-1]
    out = jnp.einsum('bnkgqs,bnskd->bnqkgd', probs.astype(v.dtype), v_band)
    return out.reshape(b, s, N_Q_HEADS * HEAD_DIM)


def rotate_every_two(x):
    x1 = x[..., 0::2]
    x2 = x[..., 1::2]
    return jnp.stack((-x2, x1), axis=-1).reshape(x.shape)


def retention_chunkwise(q, k, v):
    b, s, _, _ = q.shape
    n = s // RET_CHUNK
    q = q.astype(jnp.float32)
    k = k.astype(jnp.float32) * (RET_QK_DIM ** -0.5)
    v = v.astype(jnp.float32)
    pos = jnp.arange(s, dtype=jnp.float32)
    theta = 1.0 / (RET_ROT_BASE ** jnp.linspace(0.0, 1.0, RET_QK_DIM // 2, dtype=jnp.float32))
    ang = jnp.repeat(pos[:, None] * theta[None, :], 2, axis=-1)[None, :, None, :]
    cos, sin = jnp.cos(ang), jnp.sin(ang)
    q = q * cos + rotate_every_two(q) * sin
    k = k * cos + rotate_every_two(k) * sin
    log_gamma = jnp.log(1.0 - 2.0 ** (-5.0 - jnp.arange(RET_HEADS, dtype=jnp.float32)))
    i = jnp.arange(RET_CHUNK, dtype=jnp.float32)
    diff = i[:, None] - i[None, :]
    causal = diff >= 0
    decay_inner = jnp.where(causal[None], jnp.exp(jnp.where(causal, diff, 0.0)[None] * log_gamma[:, None, None]), 0.0)
    xi = jnp.exp((i + 1.0)[None, :] * log_gamma[:, None])
    zeta = jnp.exp((RET_CHUNK - 1.0 - i)[None, :] * log_gamma[:, None])
    gamma_chunk = jnp.exp(RET_CHUNK * log_gamma)

    def to_chunks(t):
        return t.reshape(b, n, RET_CHUNK, RET_HEADS, t.shape[-1]).transpose(1, 0, 3, 2, 4)

    def step(state, qkv):
        qc, kc, vc = qkv
        inner = jnp.einsum('bhid,bhjd->bhij', qc, kc) * decay_inner
        out = jnp.einsum('bhij,bhjv->bhiv', inner, vc)
        out = out + jnp.einsum('bhid,bhdv->bhiv', qc, state) * xi[None, :, :, None]
        state = gamma_chunk[None, :, None, None] * state + jnp.einsum('bhjd,bhjv->bhdv', kc * zeta[None, :, :, None], vc)
        return state, out

    state0 = jnp.zeros((b, RET_HEADS, RET_QK_DIM, RET_V_DIM), jnp.float32)
    _, out = lax.scan(step, state0, (to_chunks(q), to_chunks(k), to_chunks(v)))
    return out.transpose(1, 0, 3, 2, 4).reshape(b, s, RET_HEADS, RET_V_DIM)


def setup_inputs(seed: int = 0) -> dict:
    key = jax.random.key(seed)
    ks = jax.random.split(key, 14)
    L = DEPTH
    nrm = lambda k, shape, fan_in: jax.random.normal(k, shape, jnp.float32) * (fan_in ** -0.5)
    gain = lambda k, shape: 1.0 + 0.02 * jax.random.normal(k, shape, jnp.float32)
    return {
        "x": jax.random.normal(ks[0], (BATCH, SEQ, D_MODEL), jnp.float32),
        "norm_mix_gain": gain(ks[1], (L, D_MODEL)),
        "w_in": nrm(ks[2], (L, D_MODEL, D_IN), D_MODEL),
        "q_norm_gain": gain(ks[3], (L, HEAD_DIM)),
        "k_norm_gain": gain(ks[4], (L, HEAD_DIM)),
        "attn_sinks": 0.5 * jax.random.normal(ks[5], (L, N_Q_HEADS), jnp.float32),
        "w_branch_attn": nrm(ks[6], (L, ATT_Q, D_MODEL), ATT_Q),
        "w_branch_ret": nrm(ks[7], (L, RET_V, D_MODEL), RET_V),
        "w_out": nrm(ks[8], (L, D_MODEL, D_MODEL), D_MODEL),
        "norm_ffn_gain": gain(ks[9], (L, D_MODEL)),
        "w_ffn_gate": nrm(ks[10], (L, D_MODEL, D_FF), D_MODEL),
        "w_ffn_up": nrm(ks[11], (L, D_MODEL, D_FF), D_MODEL),
        "w_ffn_down": nrm(ks[12], (L, D_FF, D_MODEL), D_FF),
    }


def reference(x, norm_mix_gain, w_in, q_norm_gain, k_norm_gain, attn_sinks, w_branch_attn, w_branch_ret, w_out, norm_ffn_gain, w_ffn_gate, w_ffn_up, w_ffn_down):
    b, s, _ = x.shape
    split_points = list(np.cumsum(IN_SPLITS)[:-1])
    for l in range(DEPTH):
        h = rmsnorm(x, norm_mix_gain[l])
        proj = jnp.einsum('bsd,de->bse', h, w_in[l])
        q_a, k_a, v_a, q_r, k_r, v_r, g_r, z_a, z_r = jnp.split(proj, split_points, axis=-1)
        attn = sliding_window_attention(
            q_a.reshape(b, s, N_Q_HEADS, HEAD_DIM),
            k_a.reshape(b, s, N_KV_HEADS, HEAD_DIM),
            v_a.reshape(b, s, N_KV_HEADS, HEAD_DIM),
            q_norm_gain[l], k_norm_gain[l], attn_sinks[l])
        ret = retention_chunkwise(
            q_r.reshape(b, s, RET_HEADS, RET_QK_DIM),
            k_r.reshape(b, s, RET_HEADS, RET_QK_DIM),
            v_r.reshape(b, s, RET_HEADS, RET_V_DIM))
        ret = (jax.nn.silu(g_r.astype(jnp.float32)) * rms_group_norm(ret).reshape(b, s, RET_V)).astype(x.dtype)
        branch_a = jnp.einsum('bse,ed->bsd', attn, w_branch_attn[l])
        branch_r = jnp.einsum('bse,ed->bsd', ret, w_branch_ret[l])
        merged = jax.nn.sigmoid(z_a) * branch_a + jax.nn.sigmoid(z_r) * branch_r
        x = x + jnp.einsum('bsd,de->bse', merged, w_out[l])
        h = rmsnorm(x, norm_ffn_gain[l])
        gate = jnp.einsum('bsd,df->bsf', h, w_ffn_gate[l])
        up = jnp.einsum('bsd,df->bsf', h, w_ffn_up[l])
        x = x + jnp.einsum('bsf,fd->bsd', jax.nn.silu(gate) * up, w_ffn_down[l])
    return x
```

```python
import functools
import math

import jax
import jax.numpy as jnp
import numpy as np
from jax import lax
from jax.experimental import pallas as pl
from jax.experimental.pallas import tpu as pltpu

F32 = jnp.float32
BF16 = jnp.bfloat16

EPS = 1e-6
HEAD_DIM = 64
GROUP = 8
WINDOW = 128
RET_QK_DIM = 256
RET_V_DIM = 512
RET_CHUNK = 128
RET_ROT_BASE = 10000.0

VMEM_LIMIT_BYTES = 56 * 1024 * 1024
LANES = 128

TM_PROJ = 512
N_CHUNK_PROJ = 512
TS_ATTN = 512
TS_RET = 512
TM_OUT = 512
FF_CHUNK = 256

NEG_BIG = -1e30


def _resident(shape):
    nd = len(shape)
    return pl.BlockSpec(shape, lambda *_: (0,) * nd, pipeline_mode=pl.Buffered(1))


def _in_proj_kernel(x_ref, gain_ref, *refs):
    n = len(refs) // 2
    w_refs, o_refs = refs[:n], refs[n:]
    x = x_ref[...]
    ms = jnp.mean(x * x, axis=-1, keepdims=True)
    h = (x * lax.rsqrt(ms + EPS) * gain_ref[...]).astype(BF16)
    for w_ref, o_ref in zip(w_refs, o_refs):
        width = w_ref.shape[1]
        for c in range(0, width, N_CHUNK_PROJ):
            cw = min(N_CHUNK_PROJ, width - c)
            o_ref[:, c:c + cw] = jnp.dot(
                h, w_ref[:, c:c + cw], preferred_element_type=F32).astype(BF16)


def _in_proj(x2d, gain, weights):
    t, d = x2d.shape
    widths = [w.shape[1] for w in weights]
    return pl.pallas_call(
        _in_proj_kernel,
        grid=(t // TM_PROJ,),
        in_specs=[pl.BlockSpec((TM_PROJ, d), lambda i: (i, 0)), _resident((1, d))]
        + [_resident(w.shape) for w in weights],
        out_specs=[pl.BlockSpec((TM_PROJ, w), lambda i: (i, 0)) for w in widths],
        out_shape=[jax.ShapeDtypeStruct((t, w), BF16) for w in widths],
        compiler_params=pltpu.CompilerParams(
            dimension_semantics=("parallel",), vmem_limit_bytes=VMEM_LIMIT_BYTES),
        name="in_proj",
    )(x2d, gain, *weights)


def _attn_kernel(sink_ref, q_ref, kv_ref, kvp_ref, qg_ref, kg_ref, bd_ref, o_ref):
    n = pl.program_id(1)
    nsub = TS_ATTN // WINDOW
    bd = bd_ref[...]

    def head_mean_sq(t):
        sq = (t * t).astype(BF16)
        return jnp.dot(sq, bd, preferred_element_type=F32) * (1.0 / HEAD_DIM)

    kv_all = jnp.concatenate([kvp_ref[...], kv_ref[...]], axis=0).astype(F32)
    k_all = kv_all[:, :LANES]
    v_all = kv_all[:, LANES:]
    k_all = k_all * lax.rsqrt(head_mean_sq(k_all) + EPS) * kg_ref[...]
    lane = lax.broadcasted_iota(jnp.int32, k_all.shape, 1)
    k_sw = pltpu.roll(k_all, HEAD_DIM, axis=1)
    v_sw = pltpu.roll(v_all, HEAD_DIM, axis=1)
    lo = lane < HEAD_DIM
    zero = jnp.zeros_like(k_all)
    k_top = [jnp.where(lo, k_all, zero).astype(BF16), jnp.where(lo, k_sw, zero).astype(BF16)]
    k_bot = [jnp.where(lo, zero, k_sw).astype(BF16), jnp.where(lo, zero, k_all).astype(BF16)]
    v_top = [jnp.where(lo, v_all, zero).astype(BF16), jnp.where(lo, v_sw, zero).astype(BF16)]
    v_bot = [jnp.where(lo, zero, v_sw).astype(BF16), jnp.where(lo, zero, v_all).astype(BF16)]

    row = lax.broadcasted_iota(jnp.int32, (WINDOW, 2 * WINDOW), 0)
    col = lax.broadcasted_iota(jnp.int32, (WINDOW, 2 * WINDOW), 1)
    band = (col > row) & (col <= row + WINDOW)
    band_first = band & (col >= jnp.where(n == 0, WINDOW, 0))

    for j in range(nsub):
        q = q_ref[j * WINDOW:(j + 1) * WINDOW, :].astype(F32)
        ms = jnp.concatenate(
            [head_mean_sq(q[:, p * LANES:(p + 1) * LANES]) for p in range(q.shape[1] // LANES)], axis=1)
        qn = (q * lax.rsqrt(ms + EPS) * qg_ref[...]).astype(BF16)
        mask1 = band_first if j == 0 else band
        mask = jnp.concatenate([mask1, mask1], axis=1)
        r0 = j * WINDOW
        outs = []
        for p in range(q.shape[1] // LANES):
            g = (2 * p) // GROUP
            kb = jnp.concatenate([k_top[g][r0:r0 + 2 * WINDOW], k_bot[g][r0:r0 + 2 * WINDOW]], axis=0)
            vb = jnp.concatenate([v_top[g][r0:r0 + 2 * WINDOW], v_bot[g][r0:r0 + 2 * WINDOW]], axis=0)
            s = lax.dot_general(qn[:, p * LANES:(p + 1) * LANES], kb,
                                (((1,), (1,)), ((), ())), preferred_element_type=F32)
            s = jnp.where(mask, s, NEG_BIG)
            halves = []
            inv = []
            for e in range(2):
                sh = s[:, e * 2 * WINDOW:(e + 1) * 2 * WINDOW]
                sink = sink_ref[2 * p + e]
                m = jnp.maximum(jnp.max(sh, axis=-1, keepdims=True), sink)
                ph = jnp.exp(sh - m)
                den = jnp.sum(ph, axis=-1, keepdims=True) + jnp.exp(sink - m)
                halves.append(ph.astype(BF16))
                inv.append(jnp.broadcast_to(1.0 / den, (WINDOW, HEAD_DIM)))
            pr = jnp.concatenate(halves, axis=1)
            o = jnp.dot(pr, vb, preferred_element_type=F32)
            outs.append(o * jnp.concatenate(inv, axis=1))
        o_ref[j * WINDOW:(j + 1) * WINDOW, :] = jnp.concatenate(outs, axis=1).astype(o_ref.dtype)


def _attention(qkv, sinks, q_gain_row, k_gain_row, batch, seq):
    t = qkv.shape[0]
    d_q = q_gain_row.shape[1]
    steps = seq // TS_ATTN
    sub = TS_ATTN // WINDOW
    kv_blk = d_q // (2 * LANES)
    ii = np.arange(LANES)
    bd = jnp.asarray((ii[:, None] // HEAD_DIM) == (ii[None, :] // HEAD_DIM), BF16)
    return pl.pallas_call(
        _attn_kernel,
        grid=(batch, steps),
        in_specs=[
            pl.BlockSpec(memory_space=pltpu.SMEM),
            pl.BlockSpec((TS_ATTN, d_q), lambda b, n: (b * steps + n, 0)),
            pl.BlockSpec((TS_ATTN, 2 * LANES), lambda b, n: (b * steps + n, kv_blk)),
            pl.BlockSpec((WINDOW, 2 * LANES),
                         lambda b, n: ((b * steps + n) * sub - jnp.minimum(n, 1), kv_blk)),
            _resident((1, d_q)),
            _resident((1, LANES)),
            _resident((LANES, LANES)),
        ],
        out_specs=pl.BlockSpec((TS_ATTN, d_q), lambda b, n: (b * steps + n, 0)),
        out_shape=jax.ShapeDtypeStruct((t, d_q), BF16),
        compiler_params=pltpu.CompilerParams(
            dimension_semantics=("parallel", "parallel"), vmem_limit_bytes=VMEM_LIMIT_BYTES),
        name="swa_attention",
    )(sinks, qkv, qkv, qkv, q_gain_row, k_gain_row, bd)


def _ret_kernel(q_ref, k_ref, v_ref, g_ref, cos_ref, sin_ref, xi_ref, ginv_ref, gchunk_ref,
                o_ref, state_ref):
    n_heads = state_ref.shape[0]
    half = RET_QK_DIM // 2

    @pl.when(pl.program_id(1) == 0)
    def _():
        state_ref[...] = jnp.zeros_like(state_ref)

    ii = lax.broadcasted_iota(jnp.int32, (RET_CHUNK, RET_CHUNK), 0)
    jj = lax.broadcasted_iota(jnp.int32, (RET_CHUNK, RET_CHUNK), 1)
    causal = ii >= jj

    def rotate(t, c, s):
        t1, t2 = t[:, :half], t[:, half:]
        return jnp.concatenate([t1 * c - t2 * s, t2 * c + t1 * s], axis=1)

    def chunk(ci, carry):
        r = pl.multiple_of(ci * RET_CHUNK, RET_CHUNK)
        c = cos_ref[pl.ds(r, RET_CHUNK), :]
        s = sin_ref[pl.ds(r, RET_CHUNK), :]
        for h in range(n_heads):
            q = q_ref[pl.ds(r, RET_CHUNK), h * RET_QK_DIM:(h + 1) * RET_QK_DIM].astype(F32)
            k = k_ref[pl.ds(r, RET_CHUNK), h * RET_QK_DIM:(h + 1) * RET_QK_DIM].astype(F32)
            v = v_ref[pl.ds(r, RET_CHUNK), h * RET_V_DIM:(h + 1) * RET_V_DIM]
            qx = (rotate(q, c, s) * xi_ref[h]).astype(BF16)
            kinv = (rotate(k, c, s) * ginv_ref[h]).astype(BF16)
            inner = lax.dot_general(qx, kinv, (((1,), (1,)), ((), ())), preferred_element_type=F32)
            inner = jnp.where(causal, inner, 0.0).astype(BF16)
            state = state_ref[h]
            lhs = jnp.concatenate([inner, qx], axis=1)
            rhs = jnp.concatenate([v, state.astype(BF16)], axis=0)
            out = jnp.dot(lhs, rhs, preferred_element_type=F32)
            upd = lax.dot_general(kinv, v, (((0,), (0,)), ((), ())), preferred_element_type=F32)
            state_ref[h] = gchunk_ref[h] * (state + upd)
            normed = out * lax.rsqrt(jnp.mean(out * out, axis=-1, keepdims=True) + EPS)
            gate = g_ref[pl.ds(r, RET_CHUNK), h * RET_V_DIM:(h + 1) * RET_V_DIM].astype(F32)
            o_ref[pl.ds(r, RET_CHUNK), h * RET_V_DIM:(h + 1) * RET_V_DIM] = (
                gate * jax.nn.sigmoid(gate) * normed).astype(o_ref.dtype)
        return carry

    lax.fori_loop(0, TS_RET // RET_CHUNK, chunk, 0)


def _retention(q_r, k_r, v_r, g_r, batch, seq):
    t = q_r.shape[0]
    n_heads = q_r.shape[1] // RET_QK_DIM
    steps = seq // TS_RET
    half = RET_QK_DIM // 2
    pos = jnp.arange(seq, dtype=F32)
    theta = 1.0 / (RET_ROT_BASE ** jnp.linspace(0.0, 1.0, half, dtype=F32))
    ang = pos[:, None] * theta[None, :]
    cos, sin = jnp.cos(ang), jnp.sin(ang)
    log_gamma = jnp.log(1.0 - 2.0 ** (-5.0 - jnp.arange(n_heads, dtype=F32)))
    i = jnp.arange(RET_CHUNK, dtype=F32)
    xi = jnp.exp((i + 1.0)[None, :] * log_gamma[:, None])[:, :, None]
    ginv = (jnp.exp(-(i + 1.0)[None, :] * log_gamma[:, None]) * (RET_QK_DIM ** -0.5))[:, :, None]
    gchunk = jnp.exp(RET_CHUNK * log_gamma)

    tok = lambda w: pl.BlockSpec((TS_RET, w), lambda b, n: (b * steps + n, 0))
    tab = pl.BlockSpec((TS_RET, half), lambda b, n: (n, 0))
    return pl.pallas_call(
        _ret_kernel,
        grid=(batch, steps),
        in_specs=[tok(q_r.shape[1]), tok(k_r.shape[1]), tok(v_r.shape[1]), tok(g_r.shape[1]),
                  tab, tab, _resident(xi.shape), _resident(ginv.shape),
                  pl.BlockSpec(memory_space=pltpu.SMEM)],
        out_specs=tok(v_r.shape[1]),
        out_shape=jax.ShapeDtypeStruct((t, v_r.shape[1]), BF16),
        scratch_shapes=[pltpu.VMEM((n_heads, RET_QK_DIM, RET_V_DIM), F32)],
        compiler_params=pltpu.CompilerParams(
            dimension_semantics=("parallel", "arbitrary"), vmem_limit_bytes=VMEM_LIMIT_BYTES),
        name="retention",
    )(q_r, k_r, v_r, g_r, cos, sin, xi, ginv, gchunk)


def _out_kernel(x_ref, attn_ref, ret_ref, za_ref, zr_ref, wba_ref, wbr_ref, wo_ref, gain_ref,
                wg_ref, wu_ref, wd_ref, o_ref):
    ba = jnp.dot(attn_ref[...], wba_ref[...], preferred_element_type=F32)
    br = jnp.dot(ret_ref[...], wbr_ref[...], preferred_element_type=F32)
    merged = (jax.nn.sigmoid(za_ref[...].astype(F32)) * ba
              + jax.nn.sigmoid(zr_ref[...].astype(F32)) * br)
    x1 = x_ref[...] + jnp.dot(merged.astype(BF16), wo_ref[...], preferred_element_type=F32)
    ms = jnp.mean(x1 * x1, axis=-1, keepdims=True)
    h = (x1 * lax.rsqrt(ms + EPS) * gain_ref[...]).astype(BF16)
    acc = x1
    d_ff = wg_ref.shape[1]
    for c in range(0, d_ff, FF_CHUNK):
        gate = jnp.dot(h, wg_ref[:, c:c + FF_CHUNK], preferred_element_type=F32)
        up = jnp.dot(h, wu_ref[:, c:c + FF_CHUNK], preferred_element_type=F32)
        act = (gate * jax.nn.sigmoid(gate) * up).astype(BF16)
        acc = acc + jnp.dot(act, wd_ref[c:c + FF_CHUNK, :], preferred_element_type=F32)
    o_ref[...] = acc


def _merge_ffn(x2d, attn, ret, z_a, z_r, wba, wbr, wo, gain, wg, wu, wd):
    t, d = x2d.shape
    tok = lambda w: pl.BlockSpec((TM_OUT, w), lambda i: (i, 0))
    return pl.pallas_call(
        _out_kernel,
        grid=(t // TM_OUT,),
        in_specs=[tok(d), tok(attn.shape[1]), tok(ret.shape[1]), tok(d), tok(d),
                  _resident(wba.shape), _resident(wbr.shape), _resident(wo.shape),
                  _resident(gain.shape), _resident(wg.shape), _resident(wu.shape),
                  _resident(wd.shape)],
        out_specs=tok(d),
        out_shape=jax.ShapeDtypeStruct((t, d), F32),
        compiler_params=pltpu.CompilerParams(
            dimension_semantics=("parallel",), vmem_limit_bytes=VMEM_LIMIT_BYTES),
        name="merge_ffn",
    )(x2d, attn, ret, z_a, z_r, wba, wbr, wo, gain, wg, wu, wd)


def _deinterleave_heads(w, n_heads, head_dim):
    d = w.shape[0]
    w = w.reshape(d, n_heads, head_dim // 2, 2)
    return jnp.swapaxes(w, 2, 3).reshape(d, n_heads * head_dim)


def kernel(x, norm_mix_gain, w_in, q_norm_gain, k_norm_gain, attn_sinks, w_branch_attn, w_branch_ret,
           w_out, norm_ffn_gain, w_ffn_gate, w_ffn_up, w_ffn_down):
    batch, seq, d = x.shape
    depth = w_in.shape[0]
    att_q = w_branch_attn.shape[1]
    ret_v = w_branch_ret.shape[1]
    n_q_heads = att_q // HEAD_DIM
    att_kv = (n_q_heads // GROUP) * HEAD_DIM
    ret_heads = ret_v // RET_V_DIM
    ret_qk = ret_heads * RET_QK_DIM
    splits = np.cumsum([att_q, att_kv, att_kv, ret_qk, ret_qk, ret_v, ret_v, d])
    assert att_kv == LANES and w_in.shape[2] == splits[-1] + d

    x2d = x.reshape(batch * seq, d)
    for l in range(depth):
        wq_a, wk_a, wv_a, wq_r, wk_r, wv_r, wg_r, wz_a, wz_r = jnp.split(w_in[l], splits, axis=1)
        weights = [
            jnp.concatenate([wq_a, wk_a, wv_a], axis=1),
            _deinterleave_heads(wq_r, ret_heads, RET_QK_DIM),
            _deinterleave_heads(wk_r, ret_heads, RET_QK_DIM),
            wv_r, wg_r, wz_a, wz_r,
        ]
        weights = [w.astype(BF16) for w in weights]
        qkv_a, q_r, k_r, v_r, g_r, z_a, z_r = _in_proj(x2d, norm_mix_gain[l][None, :], weights)

        q_gain_row = jnp.tile(q_norm_gain[l] * (HEAD_DIM ** -0.5), n_q_heads)[None, :]
        k_gain_row = jnp.tile(k_norm_gain[l], LANES // HEAD_DIM)[None, :]
        attn = _attention(qkv_a, attn_sinks[l], q_gain_row, k_gain_row, batch, seq)
        ret = _retention(q_r, k_r, v_r, g_r, batch, seq)

        x2d = _merge_ffn(
            x2d, attn, ret, z_a, z_r,
            w_branch_attn[l].astype(BF16), w_branch_ret[l].astype(BF16), w_out[l].astype(BF16),
            norm_ffn_gain[l][None, :],
            w_ffn_gate[l].astype(BF16), w_ffn_up[l].astype(BF16), w_ffn_down[l].astype(BF16))
    return x2d.reshape(batch, seq, d)
```

```python
import functools
import math

import jax
import jax.numpy as jnp
import numpy as np
from jax import lax
from jax.experimental import pallas as pl
from jax.experimental.pallas import tpu as pltpu

F32 = jnp.float32
BF16 = jnp.bfloat16

EPS = 1e-6
HEAD_DIM = 64
GROUP = 8
WINDOW = 128
RET_QK_DIM = 256
RET_V_DIM = 512
RET_CHUNK = 128
RET_ROT_BASE = 10000.0

VMEM_LIMIT_BYTES = 56 * 1024 * 1024
LANES = 128

TM_PROJ = 512
N_CHUNK_PROJ = 512
TS_ATTN = 512
TS_RET = 512
TM_OUT = 512
FF_CHUNK = 256

NEG_BIG = -1e30
LOG2E = math.log2(math.e)


def _resident(shape):
    nd = len(shape)
    return pl.BlockSpec(shape, lambda *_: (0,) * nd, pipeline_mode=pl.Buffered(1))


def _in_proj_kernel(silu_flags, x_ref, gain_ref, *refs):
    n = len(refs) // 2
    w_refs, o_refs = refs[:n], refs[n:]
    x = x_ref[...]
    ms = jnp.mean(x * x, axis=-1, keepdims=True)
    h = (x * lax.rsqrt(ms + EPS) * gain_ref[...]).astype(BF16)
    for w_ref, o_ref, silu in zip(w_refs, o_refs, silu_flags):
        width = w_ref.shape[1]
        for c in range(0, width, N_CHUNK_PROJ):
            cw = min(N_CHUNK_PROJ, width - c)
            y = jnp.dot(h, w_ref[:, c:c + cw], preferred_element_type=F32)
            if silu:
                y = y * jax.nn.sigmoid(y)
            o_ref[:, c:c + cw] = y.astype(BF16)


def _in_proj(x2d, gain, weights, silu_flags):
    t, d = x2d.shape
    widths = [w.shape[1] for w in weights]
    return pl.pallas_call(
        functools.partial(_in_proj_kernel, tuple(silu_flags)),
        grid=(t // TM_PROJ,),
        in_specs=[pl.BlockSpec((TM_PROJ, d), lambda i: (i, 0)), _resident((1, d))]
        + [_resident(w.shape) for w in weights],
        out_specs=[pl.BlockSpec((TM_PROJ, w), lambda i: (i, 0)) for w in widths],
        out_shape=[jax.ShapeDtypeStruct((t, w), BF16) for w in widths],
        compiler_params=pltpu.CompilerParams(
            dimension_semantics=("parallel",), vmem_limit_bytes=VMEM_LIMIT_BYTES),
        name="in_proj",
    )(x2d, gain, *weights)


def _attn_kernel(sink_ref, q_ref, kv_ref, kvp_ref, kg_ref, bd_ref, ind_ref, exp_ref, o_ref):
    n = pl.program_id(1)
    nsub = TS_ATTN // WINDOW
    n_pairs = q_ref.shape[1] // LANES
    pairs_per_group = GROUP // 2

    kv_all = jnp.concatenate([kvp_ref[...], kv_ref[...]], axis=0).astype(F32)
    k_all = kv_all[:, :LANES]
    v_all = kv_all[:, LANES:]
    k_ms = jnp.dot((k_all * k_all).astype(BF16), bd_ref[...], preferred_element_type=F32) * (1.0 / HEAD_DIM)
    k_all = k_all * lax.rsqrt(k_ms + EPS) * kg_ref[...]
    lane = lax.broadcasted_iota(jnp.int32, k_all.shape, 1)
    lo = lane < HEAD_DIM
    k_sw = pltpu.roll(k_all, HEAD_DIM, axis=1)
    v_sw = pltpu.roll(v_all, HEAD_DIM, axis=1)
    zero = jnp.zeros_like(k_all)
    one = jnp.ones_like(k_all)
    k_top = [jnp.where(lo, k_all, zero).astype(BF16), jnp.where(lo, k_sw, zero).astype(BF16)]
    k_bot = [jnp.where(lo, zero, k_sw).astype(BF16), jnp.where(lo, zero, k_all).astype(BF16)]
    ones_lo = jnp.where(lo, one, zero)
    ones_hi = jnp.where(lo, zero, one)
    v_top = [jnp.concatenate([jnp.where(lo, v_all, zero), ones_lo], axis=1).astype(BF16),
             jnp.concatenate([jnp.where(lo, v_sw, zero), ones_lo], axis=1).astype(BF16)]
    v_bot = [jnp.concatenate([jnp.where(lo, zero, v_sw), ones_hi], axis=1).astype(BF16),
             jnp.concatenate([jnp.where(lo, zero, v_all), ones_hi], axis=1).astype(BF16)]

    row = lax.broadcasted_iota(jnp.int32, (WINDOW, WINDOW), 0)
    col = lax.broadcasted_iota(jnp.int32, (WINDOW, WINDOW), 1)
    upper = col > row
    upper_bf = upper.astype(F32).astype(BF16)
    prev_pen = jnp.where(n == 0, NEG_BIG, 0.0).astype(F32)

    for j in range(nsub):
        q = q_ref[j * WINDOW:(j + 1) * WINDOW, :]
        ssq = jnp.dot(q * q, ind_ref[...], preferred_element_type=F32)
        r = lax.rsqrt(ssq * (1.0 / HEAD_DIM) + EPS).astype(BF16)
        qn = q * jnp.dot(r, exp_ref[...], preferred_element_type=F32).astype(BF16)
        r0 = j * WINDOW
        outs = []
        for g in range(n_pairs // pairs_per_group):
            p0 = g * pairs_per_group
            qs = jnp.concatenate(
                [qn[:, (p0 + pp) * LANES:(p0 + pp + 1) * LANES] for pp in range(pairs_per_group)], axis=0)
            kb = jnp.concatenate([k_top[g][r0:r0 + 2 * WINDOW], k_bot[g][r0:r0 + 2 * WINDOW]], axis=0)
            vb = jnp.concatenate([v_top[g][r0:r0 + 2 * WINDOW], v_bot[g][r0:r0 + 2 * WINDOW]], axis=0)
            s = lax.dot_general(qs, kb, (((1,), (1,)), ((), ())), preferred_element_type=F32)
            prob_rows = []
            sink_rows = []
            for pp in range(pairs_per_group):
                rs = slice(pp * WINDOW, (pp + 1) * WINDOW)
                cols, sinks = [], []
                for e in range(2):
                    sp = s[rs, e * 2 * WINDOW:e * 2 * WINDOW + WINDOW]
                    sc = s[rs, e * 2 * WINDOW + WINDOW:(e + 1) * 2 * WINDOW]
                    if j == 0:
                        sp = sp + prev_pen
                    comb = jnp.where(upper, sp, sc)
                    sink = sink_ref[2 * (p0 + pp) + e]
                    m = jnp.maximum(jnp.max(comb, axis=-1, keepdims=True), sink)
                    p = jnp.exp2(comb - m).astype(BF16)
                    p_prev = p * upper_bf
                    cols += [p_prev, p - p_prev]
                    sinks.append(jnp.broadcast_to(jnp.exp2(sink - m), (WINDOW, HEAD_DIM)))
                prob_rows.append(jnp.concatenate(cols, axis=1))
                sink_rows.append(jnp.concatenate(sinks, axis=1))
            pr = jnp.concatenate(prob_rows, axis=0)
            oe = jnp.dot(pr, vb, preferred_element_type=F32)
            o = oe[:, :LANES] / (oe[:, LANES:] + jnp.concatenate(sink_rows, axis=0))
            outs += [o[pp * WINDOW:(pp + 1) * WINDOW] for pp in range(pairs_per_group)]
        o_ref[j * WINDOW:(j + 1) * WINDOW, :] = jnp.concatenate(outs, axis=1).astype(o_ref.dtype)


def _attention(qkv, sinks, k_gain_row, batch, seq, d_q):
    t = qkv.shape[0]
    steps = seq // TS_ATTN
    sub = TS_ATTN // WINDOW
    kv_blk = d_q // (2 * LANES)
    ii = np.arange(LANES)
    dd = np.arange(d_q)
    bd = jnp.asarray((ii[:, None] // HEAD_DIM) == (ii[None, :] // HEAD_DIM), BF16)
    ind = jnp.asarray((dd[:, None] // HEAD_DIM) == ii[None, :], BF16)
    expand = jnp.asarray(ii[:, None] == (dd[None, :] // HEAD_DIM), BF16)
    return pl.pallas_call(
        _attn_kernel,
        grid=(batch, steps),
        in_specs=[
            pl.BlockSpec(memory_space=pltpu.SMEM),
            pl.BlockSpec((TS_ATTN, d_q), lambda b, n: (b * steps + n, 0)),
            pl.BlockSpec((TS_ATTN, 2 * LANES), lambda b, n: (b * steps + n, kv_blk)),
            pl.BlockSpec((WINDOW, 2 * LANES),
                         lambda b, n: ((b * steps + n) * sub - jnp.minimum(n, 1), kv_blk)),
            _resident((1, LANES)),
            _resident((LANES, LANES)),
            _resident((d_q, LANES)),
            _resident((LANES, d_q)),
        ],
        out_specs=pl.BlockSpec((TS_ATTN, d_q), lambda b, n: (b * steps + n, 0)),
        out_shape=jax.ShapeDtypeStruct((t, d_q), BF16),
        compiler_params=pltpu.CompilerParams(
            dimension_semantics=("parallel", "parallel"), vmem_limit_bytes=VMEM_LIMIT_BYTES),
        name="swa_attention",
    )(sinks, qkv, qkv, qkv, k_gain_row, bd, ind, expand)


def _ret_kernel(q_ref, k_ref, v_ref, g_ref, cos_ref, sin_ref, xi_ref, ginv_ref, gchunk_ref,
                o_ref, state_ref):
    n_heads = state_ref.shape[0]
    half = RET_QK_DIM // 2

    @pl.when(pl.program_id(1) == 0)
    def _():
        state_ref[...] = jnp.zeros_like(state_ref)

    ii = lax.broadcasted_iota(jnp.int32, (RET_CHUNK, RET_CHUNK), 0)
    jj = lax.broadcasted_iota(jnp.int32, (RET_CHUNK, RET_CHUNK), 1)
    causal = ii >= jj

    def rotate(t, c, s):
        t1, t2 = t[:, :half], t[:, half:]
        return jnp.concatenate([t1 * c - t2 * s, t2 * c + t1 * s], axis=1)

    def chunk(ci, carry):
        r = pl.multiple_of(ci * RET_CHUNK, RET_CHUNK)
        c = cos_ref[pl.ds(r, RET_CHUNK), :]
        s = sin_ref[pl.ds(r, RET_CHUNK), :]
        for h in range(n_heads):
            q = q_ref[pl.ds(r, RET_CHUNK), h * RET_QK_DIM:(h + 1) * RET_QK_DIM].astype(F32)
            k = k_ref[pl.ds(r, RET_CHUNK), h * RET_QK_DIM:(h + 1) * RET_QK_DIM].astype(F32)
            v = v_ref[pl.ds(r, RET_CHUNK), h * RET_V_DIM:(h + 1) * RET_V_DIM]
            qx = (rotate(q, c, s) * xi_ref[h]).astype(BF16)
            kinv = (rotate(k, c, s) * ginv_ref[h]).astype(BF16)
            inner = lax.dot_general(qx, kinv, (((1,), (1,)), ((), ())), preferred_element_type=F32)
            inner = jnp.where(causal, inner, 0.0).astype(BF16)
            state = state_ref[h]
            lhs = jnp.concatenate([inner, qx], axis=1)
            rhs = jnp.concatenate([v, state.astype(BF16)], axis=0)
            out = jnp.dot(lhs, rhs, preferred_element_type=F32)
            upd = lax.dot_general(kinv, v, (((0,), (0,)), ((), ())), preferred_element_type=F32)
            state_ref[h] = gchunk_ref[h] * (state + upd)
            normed = out * lax.rsqrt(jnp.mean(out * out, axis=-1, keepdims=True) + EPS)
            gate = g_ref[pl.ds(r, RET_CHUNK), h * RET_V_DIM:(h + 1) * RET_V_DIM].astype(F32)
            o_ref[pl.ds(r, RET_CHUNK), h * RET_V_DIM:(h + 1) * RET_V_DIM] = (
                gate * normed).astype(o_ref.dtype)
        return carry

    lax.fori_loop(0, TS_RET // RET_CHUNK, chunk, 0, unroll=2)


def _retention(q_r, k_r, v_r, gate_r, batch, seq):
    t = q_r.shape[0]
    n_heads = q_r.shape[1] // RET_QK_DIM
    steps = seq // TS_RET
    half = RET_QK_DIM // 2
    pos = jnp.arange(seq, dtype=F32)
    theta = 1.0 / (RET_ROT_BASE ** jnp.linspace(0.0, 1.0, half, dtype=F32))
    ang = pos[:, None] * theta[None, :]
    cos, sin = jnp.cos(ang), jnp.sin(ang)
    log_gamma = jnp.log(1.0 - 2.0 ** (-5.0 - jnp.arange(n_heads, dtype=F32)))
    i = jnp.arange(RET_CHUNK, dtype=F32)
    xi = jnp.exp((i + 1.0)[None, :] * log_gamma[:, None])[:, :, None]
    ginv = (jnp.exp(-(i + 1.0)[None, :] * log_gamma[:, None]) * (RET_QK_DIM ** -0.5))[:, :, None]
    gchunk = jnp.exp(RET_CHUNK * log_gamma)

    tok = lambda w: pl.BlockSpec((TS_RET, w), lambda b, n: (b * steps + n, 0))
    tab = pl.BlockSpec((TS_RET, half), lambda b, n: (n, 0))
    return pl.pallas_call(
        _ret_kernel,
        grid=(batch, steps),
        in_specs=[tok(q_r.shape[1]), tok(k_r.shape[1]), tok(v_r.shape[1]), tok(gate_r.shape[1]),
                  tab, tab, _resident(xi.shape), _resident(ginv.shape),
                  pl.BlockSpec(memory_space=pltpu.SMEM)],
        out_specs=tok(v_r.shape[1]),
        out_shape=jax.ShapeDtypeStruct((t, v_r.shape[1]), BF16),
        scratch_shapes=[pltpu.VMEM((n_heads, RET_QK_DIM, RET_V_DIM), F32)],
        compiler_params=pltpu.CompilerParams(
            dimension_semantics=("parallel", "arbitrary"), vmem_limit_bytes=VMEM_LIMIT_BYTES),
        name="retention",
    )(q_r, k_r, v_r, gate_r, cos, sin, xi, ginv, gchunk)


def _out_kernel(x_ref, attn_ref, ret_ref, za_ref, zr_ref, wba_ref, wbr_ref, wo_ref, gain_ref,
                wg_ref, wu_ref, wd_ref, o_ref):
    ba = jnp.dot(attn_ref[...], wba_ref[...], preferred_element_type=F32)
    br = jnp.dot(ret_ref[...], wbr_ref[...], preferred_element_type=F32)
    merged = (jax.nn.sigmoid(za_ref[...].astype(F32)) * ba
              + jax.nn.sigmoid(zr_ref[...].astype(F32)) * br)
    x1 = x_ref[...] + jnp.dot(merged.astype(BF16), wo_ref[...], preferred_element_type=F32)
    ms = jnp.mean(x1 * x1, axis=-1, keepdims=True)
    h = (x1 * lax.rsqrt(ms + EPS) * gain_ref[...]).astype(BF16)
    acc = x1
    d_ff = wg_ref.shape[1]
    for c in range(0, d_ff, FF_CHUNK):
        gate = jnp.dot(h, wg_ref[:, c:c + FF_CHUNK], preferred_element_type=F32)
        up = jnp.dot(h, wu_ref[:, c:c + FF_CHUNK], preferred_element_type=F32)
        act = (gate * jax.nn.sigmoid(gate) * up).astype(BF16)
        acc = acc + jnp.dot(act, wd_ref[c:c + FF_CHUNK, :], preferred_element_type=F32)
    o_ref[...] = acc


def _merge_ffn(x2d, attn, ret, z_a, z_r, wba, wbr, wo, gain, wg, wu, wd):
    t, d = x2d.shape
    tok = lambda w: pl.BlockSpec((TM_OUT, w), lambda i: (i, 0))
    return pl.pallas_call(
        _out_kernel,
        grid=(t // TM_OUT,),
        in_specs=[tok(d), tok(attn.shape[1]), tok(ret.shape[1]), tok(d), tok(d),
                  _resident(wba.shape), _resident(wbr.shape), _resident(wo.shape),
                  _resident(gain.shape), _resident(wg.shape), _resident(wu.shape),
                  _resident(wd.shape)],
        out_specs=tok(d),
        out_shape=jax.ShapeDtypeStruct((t, d), F32),
        compiler_params=pltpu.CompilerParams(
            dimension_semantics=("parallel",), vmem_limit_bytes=VMEM_LIMIT_BYTES),
        name="merge_ffn",
    )(x2d, attn, ret, z_a, z_r, wba, wbr, wo, gain, wg, wu, wd)


def _deinterleave_heads(w, n_heads, head_dim):
    d = w.shape[0]
    w = w.reshape(d, n_heads, head_dim // 2, 2)
    return jnp.swapaxes(w, 2, 3).reshape(d, n_heads * head_dim)


def kernel(x, norm_mix_gain, w_in, q_norm_gain, k_norm_gain, attn_sinks, w_branch_attn, w_branch_ret,
           w_out, norm_ffn_gain, w_ffn_gate, w_ffn_up, w_ffn_down):
    batch, seq, d = x.shape
    depth = w_in.shape[0]
    att_q = w_branch_attn.shape[1]
    ret_v = w_branch_ret.shape[1]
    n_q_heads = att_q // HEAD_DIM
    att_kv = (n_q_heads // GROUP) * HEAD_DIM
    ret_heads = ret_v // RET_V_DIM
    ret_qk = ret_heads * RET_QK_DIM
    splits = np.cumsum([att_q, att_kv, att_kv, ret_qk, ret_qk, ret_v, ret_v, d])
    assert att_kv == LANES and w_in.shape[2] == splits[-1] + d

    x2d = x.reshape(batch * seq, d)
    for l in range(depth):
        wq_a, wk_a, wv_a, wq_r, wk_r, wv_r, wg_r, wz_a, wz_r = jnp.split(w_in[l], splits, axis=1)
        weights = [
            jnp.concatenate([wq_a, wk_a, wv_a], axis=1),
            _deinterleave_heads(wq_r, ret_heads, RET_QK_DIM),
            _deinterleave_heads(wk_r, ret_heads, RET_QK_DIM),
            wv_r, wg_r, wz_a, wz_r,
        ]
        weights = [w.astype(BF16) for w in weights]
        silu_flags = [False, False, False, False, True, False, False]
        qkv_a, q_r, k_r, v_r, gate_r, z_a, z_r = _in_proj(x2d, norm_mix_gain[l][None, :], weights, silu_flags)

        k_gain_row = jnp.tile(k_norm_gain[l] * q_norm_gain[l] * (HEAD_DIM ** -0.5 * LOG2E),
                              LANES // HEAD_DIM)[None, :]
        attn = _attention(qkv_a, attn_sinks[l] * LOG2E, k_gain_row, batch, seq, att_q)
        ret = _retention(q_r, k_r, v_r, gate_r, batch, seq)

        x2d = _merge_ffn(
            x2d, attn, ret, z_a, z_r,
            w_branch_attn[l].astype(BF16), w_branch_ret[l].astype(BF16), w_out[l].astype(BF16),
            norm_ffn_gain[l][None, :],
            w_ffn_gate[l].astype(BF16), w_ffn_up[l].astype(BF16), w_ffn_down[l].astype(BF16))
    return x2d.reshape(batch, seq, d)
```

```python
import functools
import math

import jax
import jax.numpy as jnp
import numpy as np
from jax import lax
from jax.experimental import pallas as pl
from jax.experimental.pallas import tpu as pltpu

F32 = jnp.float32
BF16 = jnp.bfloat16

EPS = 1e-6
HEAD_DIM = 64
GROUP = 8
WINDOW = 128
RET_QK_DIM = 256
RET_V_DIM = 512
RET_CHUNK = 256
RET_ROT_BASE = 10000.0

VMEM_LIMIT_BYTES = 56 * 1024 * 1024
LANES = 128

TM_PROJ = 512
N_CHUNK_PROJ = 512
N_CHUNK_FILL = 256
TS_ATTN = 512
TM_OUT = 512
FF_CHUNK = 256

NEG_BIG = -1e30
LOG2E = math.log2(math.e)


def _resident(shape):
    nd = len(shape)
    return pl.BlockSpec(shape, lambda *_: (0,) * nd, pipeline_mode=pl.Buffered(1))


def _project_tasks(h, w_ref, o_ref, chunk, silu=False):
    def task(c, cw):
        y = jnp.dot(h, w_ref[:, c:c + cw], preferred_element_type=F32)
        if silu:
            y = y * jax.nn.sigmoid(y)
        o_ref[:, c:c + cw] = y.astype(BF16)

    width = w_ref.shape[1]
    return [functools.partial(task, c, min(chunk, width - c)) for c in range(0, width, chunk)]


def _proj_ret_kernel(x_ref, gain_ref, wq_ref, wk_ref, wv_ref, wg_ref, wa_ref, wza_ref, wzr_ref,
                     cos_ref, sin_ref, xi_ref, ginv_ref, gchunk_ref,
                     qkv_ref, za_ref, zr_ref, ret_ref,
                     state_ref, q_s, k_s, v_s, g_s):
    n_heads = state_ref.shape[0]
    half = RET_QK_DIM // 2

    @pl.when(pl.program_id(1) == 0)
    def _():
        state_ref[...] = jnp.zeros_like(state_ref)

    x = x_ref[...]
    ms = jnp.mean(x * x, axis=-1, keepdims=True)
    h = (x * lax.rsqrt(ms + EPS) * gain_ref[...]).astype(BF16)
    for task in (_project_tasks(h, wq_ref, q_s, N_CHUNK_PROJ) + _project_tasks(h, wk_ref, k_s, N_CHUNK_PROJ)
                 + _project_tasks(h, wv_ref, v_s, N_CHUNK_PROJ)
                 + _project_tasks(h, wg_ref, g_s, N_CHUNK_PROJ, silu=True)):
        task()
    fillers = (_project_tasks(h, wa_ref, qkv_ref, N_CHUNK_FILL) + _project_tasks(h, wza_ref, za_ref, N_CHUNK_FILL)
               + _project_tasks(h, wzr_ref, zr_ref, N_CHUNK_FILL))

    ii = lax.broadcasted_iota(jnp.int32, (RET_CHUNK, RET_CHUNK), 0)
    jj = lax.broadcasted_iota(jnp.int32, (RET_CHUNK, RET_CHUNK), 1)
    causal = ii >= jj

    def rotate(t, c, s):
        t1, t2 = t[:, :half], t[:, half:]
        return jnp.concatenate([t1 * c - t2 * s, t2 * c + t1 * s], axis=1)

    for ci in range(TM_PROJ // RET_CHUNK):
        rows = slice(ci * RET_CHUNK, (ci + 1) * RET_CHUNK)
        c = cos_ref[rows, :]
        s = sin_ref[rows, :]
        for hd in range(n_heads):
            qk_cols = slice(hd * RET_QK_DIM, (hd + 1) * RET_QK_DIM)
            v_cols = slice(hd * RET_V_DIM, (hd + 1) * RET_V_DIM)
            q = q_s[rows, qk_cols].astype(F32)
            k = k_s[rows, qk_cols].astype(F32)
            v = v_s[rows, v_cols]
            qx = (rotate(q, c, s) * xi_ref[hd]).astype(BF16)
            kinv = (rotate(k, c, s) * ginv_ref[hd]).astype(BF16)
            inner = lax.dot_general(qx, kinv, (((1,), (1,)), ((), ())), preferred_element_type=F32)
            inner = jnp.where(causal, inner, 0.0).astype(BF16)
            state = state_ref[hd]
            lhs = jnp.concatenate([inner, qx], axis=1)
            rhs = jnp.concatenate([v, state.astype(BF16)], axis=0)
            out = jnp.dot(lhs, rhs, preferred_element_type=F32)
            upd = lax.dot_general(kinv, v, (((0,), (0,)), ((), ())), preferred_element_type=F32)
            state_ref[hd] = gchunk_ref[hd] * (state + upd)
            normed = out * lax.rsqrt(jnp.mean(out * out, axis=-1, keepdims=True) + EPS)
            ret_ref[rows, v_cols] = (g_s[rows, v_cols].astype(F32) * normed).astype(ret_ref.dtype)
            if fillers:
                fillers.pop(0)()
    for task in fillers:
        task()


def _proj_retention(x2d, gain, w_qr, w_kr, w_vr, w_gr, w_attn, w_za, w_zr, batch, seq):
    t, d = x2d.shape
    n_heads = w_qr.shape[1] // RET_QK_DIM
    steps = seq // TM_PROJ
    half = RET_QK_DIM // 2
    pos = jnp.arange(seq, dtype=F32)
    theta = 1.0 / (RET_ROT_BASE ** jnp.linspace(0.0, 1.0, half, dtype=F32))
    ang = pos[:, None] * theta[None, :]
    cos, sin = jnp.cos(ang), jnp.sin(ang)
    log_gamma = jnp.log(1.0 - 2.0 ** (-5.0 - jnp.arange(n_heads, dtype=F32)))
    i = jnp.arange(RET_CHUNK, dtype=F32)
    xi = jnp.exp((i + 1.0)[None, :] * log_gamma[:, None])[:, :, None]
    ginv = (jnp.exp(-(i + 1.0)[None, :] * log_gamma[:, None]) * (RET_QK_DIM ** -0.5))[:, :, None]
    gchunk = jnp.exp(RET_CHUNK * log_gamma)

    weights = [w_qr, w_kr, w_vr, w_gr, w_attn, w_za, w_zr]
    tok = lambda w: pl.BlockSpec((TM_PROJ, w), lambda b, n: (b * steps + n, 0))
    tab = pl.BlockSpec((TM_PROJ, half), lambda b, n: (n, 0))
    out_widths = [w_attn.shape[1], w_za.shape[1], w_zr.shape[1], w_vr.shape[1]]
    return pl.pallas_call(
        _proj_ret_kernel,
        grid=(batch, steps),
        in_specs=[tok(d), _resident((1, d))] + [_resident(w.shape) for w in weights]
        + [tab, tab, _resident(xi.shape), _resident(ginv.shape), pl.BlockSpec(memory_space=pltpu.SMEM)],
        out_specs=[tok(w) for w in out_widths],
        out_shape=[jax.ShapeDtypeStruct((t, w), BF16) for w in out_widths],
        scratch_shapes=[pltpu.VMEM((n_heads, RET_QK_DIM, RET_V_DIM), F32)]
        + [pltpu.VMEM((TM_PROJ, w.shape[1]), BF16) for w in (w_qr, w_kr, w_vr, w_gr)],
        compiler_params=pltpu.CompilerParams(
            dimension_semantics=("parallel", "arbitrary"), vmem_limit_bytes=VMEM_LIMIT_BYTES),
        name="proj_retention",
    )(x2d, gain, *weights, cos, sin, xi, ginv, gchunk)


def _attn_kernel(sink_ref, q_ref, kv_ref, kvp_ref, kg_ref, bd_ref, ind_ref, exp_ref, o_ref):
    n = pl.program_id(1)
    nsub = TS_ATTN // WINDOW
    n_pairs = q_ref.shape[1] // LANES
    pairs_per_group = GROUP // 2

    kv_all = jnp.concatenate([kvp_ref[...], kv_ref[...]], axis=0).astype(F32)
    k_all = kv_all[:, :LANES]
    v_all = kv_all[:, LANES:]
    k_ms = jnp.dot((k_all * k_all).astype(BF16), bd_ref[...], preferred_element_type=F32) * (1.0 / HEAD_DIM)
    k_all = k_all * lax.rsqrt(k_ms + EPS) * kg_ref[...]
    lane = lax.broadcasted_iota(jnp.int32, k_all.shape, 1)
    lo = lane < HEAD_DIM
    k_sw = pltpu.roll(k_all, HEAD_DIM, axis=1)
    v_sw = pltpu.roll(v_all, HEAD_DIM, axis=1)
    zero = jnp.zeros_like(k_all)
    one = jnp.ones_like(k_all)
    k_top = [jnp.where(lo, k_all, zero).astype(BF16), jnp.where(lo, k_sw, zero).astype(BF16)]
    k_bot = [jnp.where(lo, zero, k_sw).astype(BF16), jnp.where(lo, zero, k_all).astype(BF16)]
    ones_lo = jnp.where(lo, one, zero)
    ones_hi = jnp.where(lo, zero, one)
    v_top = [jnp.concatenate([jnp.where(lo, v_all, zero), ones_lo], axis=1).astype(BF16),
             jnp.concatenate([jnp.where(lo, v_sw, zero), ones_lo], axis=1).astype(BF16)]
    v_bot = [jnp.concatenate([jnp.where(lo, zero, v_sw), ones_hi], axis=1).astype(BF16),
             jnp.concatenate([jnp.where(lo, zero, v_all), ones_hi], axis=1).astype(BF16)]

    row = lax.broadcasted_iota(jnp.int32, (WINDOW, WINDOW), 0)
    col = lax.broadcasted_iota(jnp.int32, (WINDOW, WINDOW), 1)
    upper = col > row
    upper_bf = upper.astype(F32).astype(BF16)
    prev_pen = jnp.where(n == 0, NEG_BIG, 0.0).astype(F32)

    for j in range(nsub):
        q = q_ref[j * WINDOW:(j + 1) * WINDOW, :]
        ssq = jnp.dot(q * q, ind_ref[...], preferred_element_type=F32)
        r = lax.rsqrt(ssq * (1.0 / HEAD_DIM) + EPS).astype(BF16)
        qn = q * jnp.dot(r, exp_ref[...], preferred_element_type=F32).astype(BF16)
        r0 = j * WINDOW
        outs = []
        for g in range(n_pairs // pairs_per_group):
            p0 = g * pairs_per_group
            qs = jnp.concatenate(
                [qn[:, (p0 + pp) * LANES:(p0 + pp + 1) * LANES] for pp in range(pairs_per_group)], axis=0)
            kb = jnp.concatenate([k_top[g][r0:r0 + 2 * WINDOW], k_bot[g][r0:r0 + 2 * WINDOW]], axis=0)
            vb = jnp.concatenate([v_top[g][r0:r0 + 2 * WINDOW], v_bot[g][r0:r0 + 2 * WINDOW]], axis=0)
            s = lax.dot_general(qs, kb, (((1,), (1,)), ((), ())), preferred_element_type=F32)
            prob_rows = []
            sink_rows = []
            for pp in range(pairs_per_group):
                rs = slice(pp * WINDOW, (pp + 1) * WINDOW)
                cols, sinks = [], []
                for e in range(2):
                    sp = s[rs, e * 2 * WINDOW:e * 2 * WINDOW + WINDOW]
                    sc = s[rs, e * 2 * WINDOW + WINDOW:(e + 1) * 2 * WINDOW]
                    if j == 0:
                        sp = sp + prev_pen
                    comb = jnp.where(upper, sp, sc)
                    sink = sink_ref[2 * (p0 + pp) + e]
                    m = jnp.maximum(jnp.max(comb, axis=-1, keepdims=True), sink)
                    p = jnp.exp2(comb - m).astype(BF16)
                    p_prev = p * upper_bf
                    cols += [p_prev, p - p_prev]
                    sinks.append(jnp.broadcast_to(jnp.exp2(sink - m), (WINDOW, HEAD_DIM)))
                prob_rows.append(jnp.concatenate(cols, axis=1))
                sink_rows.append(jnp.concatenate(sinks, axis=1))
            pr = jnp.concatenate(prob_rows, axis=0)
            oe = jnp.dot(pr, vb, preferred_element_type=F32)
            o = oe[:, :LANES] / (oe[:, LANES:] + jnp.concatenate(sink_rows, axis=0))
            outs += [o[pp * WINDOW:(pp + 1) * WINDOW] for pp in range(pairs_per_group)]
        o_ref[j * WINDOW:(j + 1) * WINDOW, :] = jnp.concatenate(outs, axis=1).astype(o_ref.dtype)


def _attention(qkv, sinks, k_gain_row, batch, seq, d_q):
    t = qkv.shape[0]
    steps = seq // TS_ATTN
    sub = TS_ATTN // WINDOW
    kv_blk = d_q // (2 * LANES)
    ii = np.arange(LANES)
    dd = np.arange(d_q)
    bd = jnp.asarray((ii[:, None] // HEAD_DIM) == (ii[None, :] // HEAD_DIM), BF16)
    ind = jnp.asarray((dd[:, None] // HEAD_DIM) == ii[None, :], BF16)
    expand = jnp.asarray(ii[:, None] == (dd[None, :] // HEAD_DIM), BF16)
    return pl.pallas_call(
        _attn_kernel,
        grid=(batch, steps),
        in_specs=[
            pl.BlockSpec(memory_space=pltpu.SMEM),
            pl.BlockSpec((TS_ATTN, d_q), lambda b, n: (b * steps + n, 0)),
            pl.BlockSpec((TS_ATTN, 2 * LANES), lambda b, n: (b * steps + n, kv_blk)),
            pl.BlockSpec((WINDOW, 2 * LANES),
                         lambda b, n: ((b * steps + n) * sub - jnp.minimum(n, 1), kv_blk)),
            _resident((1, LANES)),
            _resident((LANES, LANES)),
            _resident((d_q, LANES)),
            _resident((LANES, d_q)),
        ],
        out_specs=pl.BlockSpec((TS_ATTN, d_q), lambda b, n: (b * steps + n, 0)),
        out_shape=jax.ShapeDtypeStruct((t, d_q), BF16),
        compiler_params=pltpu.CompilerParams(
            dimension_semantics=("parallel", "parallel"), vmem_limit_bytes=VMEM_LIMIT_BYTES),
        name="swa_attention",
    )(sinks, qkv, qkv, qkv, k_gain_row, bd, ind, expand)


def _out_kernel(x_ref, attn_ref, ret_ref, za_ref, zr_ref, wba_ref, wbr_ref, wo_ref, gain_ref,
                wg_ref, wu_ref, wd_ref, o_ref):
    ba = jnp.dot(attn_ref[...], wba_ref[...], preferred_element_type=F32)
    br = jnp.dot(ret_ref[...], wbr_ref[...], preferred_element_type=F32)
    merged = (jax.nn.sigmoid(za_ref[...].astype(F32)) * ba
              + jax.nn.sigmoid(zr_ref[...].astype(F32)) * br)
    x1 = x_ref[...] + jnp.dot(merged.astype(BF16), wo_ref[...], preferred_element_type=F32)
    ms = jnp.mean(x1 * x1, axis=-1, keepdims=True)
    h = (x1 * lax.rsqrt(ms + EPS) * gain_ref[...]).astype(BF16)
    acc = x1
    d_ff = wg_ref.shape[1]
    for c in range(0, d_ff, FF_CHUNK):
        gate = jnp.dot(h, wg_ref[:, c:c + FF_CHUNK], preferred_element_type=F32)
        up = jnp.dot(h, wu_ref[:, c:c + FF_CHUNK], preferred_element_type=F32)
        act = (gate * jax.nn.sigmoid(gate) * up).astype(BF16)
        acc = acc + jnp.dot(act, wd_ref[c:c + FF_CHUNK, :], preferred_element_type=F32)
    o_ref[...] = acc


def _merge_ffn(x2d, attn, ret, z_a, z_r, wba, wbr, wo, gain, wg, wu, wd):
    t, d = x2d.shape
    tok = lambda w: pl.BlockSpec((TM_OUT, w), lambda i: (i, 0))
    return pl.pallas_call(
        _out_kernel,
        grid=(t // TM_OUT,),
        in_specs=[tok(d), tok(attn.shape[1]), tok(ret.shape[1]), tok(d), tok(d),
                  _resident(wba.shape), _resident(wbr.shape), _resident(wo.shape),
                  _resident(gain.shape), _resident(wg.shape), _resident(wu.shape),
                  _resident(wd.shape)],
        out_specs=tok(d),
        out_shape=jax.ShapeDtypeStruct((t, d), F32),
        compiler_params=pltpu.CompilerParams(
            dimension_semantics=("parallel",), vmem_limit_bytes=VMEM_LIMIT_BYTES),
        name="merge_ffn",
    )(x2d, attn, ret, z_a, z_r, wba, wbr, wo, gain, wg, wu, wd)


def _deinterleave_heads(w, n_heads, head_dim):
    d = w.shape[0]
    w = w.reshape(d, n_heads, head_dim // 2, 2)
    return jnp.swapaxes(w, 2, 3).reshape(d, n_heads * head_dim)


def kernel(x, norm_mix_gain, w_in, q_norm_gain, k_norm_gain, attn_sinks, w_branch_attn, w_branch_ret,
           w_out, norm_ffn_gain, w_ffn_gate, w_ffn_up, w_ffn_down):
    batch, seq, d = x.shape
    depth = w_in.shape[0]
    att_q = w_branch_attn.shape[1]
    ret_v = w_branch_ret.shape[1]
    n_q_heads = att_q // HEAD_DIM
    att_kv = (n_q_heads // GROUP) * HEAD_DIM
    ret_heads = ret_v // RET_V_DIM
    ret_qk = ret_heads * RET_QK_DIM
    splits = np.cumsum([att_q, att_kv, att_kv, ret_qk, ret_qk, ret_v, ret_v, d])
    assert att_kv == LANES and w_in.shape[2] == splits[-1] + d

    x2d = x.reshape(batch * seq, d)
    for l in range(depth):
        wq_a, wk_a, wv_a, wq_r, wk_r, wv_r, wg_r, wz_a, wz_r = jnp.split(w_in[l], splits, axis=1)
        weights = [
            _deinterleave_heads(wq_r, ret_heads, RET_QK_DIM),
            _deinterleave_heads(wk_r, ret_heads, RET_QK_DIM),
            wv_r, wg_r,
            jnp.concatenate([wq_a, wk_a, wv_a], axis=1),
            wz_a, wz_r,
        ]
        weights = [w.astype(BF16) for w in weights]
        qkv_a, z_a, z_r, ret = _proj_retention(x2d, norm_mix_gain[l][None, :], *weights, batch, seq)

        k_gain_row = jnp.tile(k_norm_gain[l] * q_norm_gain[l] * (HEAD_DIM ** -0.5 * LOG2E),
                              LANES // HEAD_DIM)[None, :]
        attn = _attention(qkv_a, attn_sinks[l] * LOG2E, k_gain_row, batch, seq, att_q)

        x2d = _merge_ffn(
            x2d, attn, ret, z_a, z_r,
            w_branch_attn[l].astype(BF16), w_branch_ret[l].astype(BF16), w_out[l].astype(BF16),
            norm_ffn_gain[l][None, :],
            w_ffn_gate[l].astype(BF16), w_ffn_up[l].astype(BF16), w_ffn_down[l].astype(BF16))
    return x2d.reshape(batch, seq, d)
```

```python
import functools
import math

import jax
import jax.numpy as jnp
import numpy as np
from jax import lax
from jax.experimental import pallas as pl
from jax.experimental.pallas import tpu as pltpu

F32 = jnp.float32
BF16 = jnp.bfloat16

EPS = 1e-6
HEAD_DIM = 64
GROUP = 8
WINDOW = 128
RET_QK_DIM = 256
RET_V_DIM = 512
RET_CHUNK = 256
RET_ROT_BASE = 10000.0

VMEM_LIMIT_BYTES = 56 * 1024 * 1024
LANES = 128

TM_PROJ = 512
N_CHUNK_PROJ = 512
N_CHUNK_FILL = 256
TS_ATTN = 2048
TM_OUT = 512
FF_CHUNK = 256

NEG_BIG = -1e30
SUM_ROWS = 16
SCORE_LOOKAHEAD = 2
LOG2E = math.log2(math.e)


def _resident(shape):
    nd = len(shape)
    return pl.BlockSpec(shape, lambda *_: (0,) * nd, pipeline_mode=pl.Buffered(1))


def _project_tasks(h, w_ref, o_ref, chunk, silu=False):
    def task(c, cw):
        y = jnp.dot(h, w_ref[:, c:c + cw], preferred_element_type=F32)
        if silu:
            y = y * jax.nn.sigmoid(y)
        o_ref[:, c:c + cw] = y.astype(BF16)

    width = w_ref.shape[1]
    return [functools.partial(task, c, min(chunk, width - c)) for c in range(0, width, chunk)]


def _proj_ret_kernel(x_ref, gain_ref, wq_ref, wk_ref, wv_ref, wg_ref, wa_ref, wza_ref, wzr_ref,
                     cos_ref, sin_ref, xi_ref, ginv_ref, gchunk_ref,
                     qkv_ref, za_ref, zr_ref, ret_ref,
                     state_ref, q_s, k_s, v_s, g_s):
    n_heads = state_ref.shape[0]
    half = RET_QK_DIM // 2

    @pl.when(pl.program_id(1) == 0)
    def _():
        state_ref[...] = jnp.zeros_like(state_ref)

    x = x_ref[...]
    ms = jnp.mean(x * x, axis=-1, keepdims=True)
    h = (x * lax.rsqrt(ms + EPS) * gain_ref[...]).astype(BF16)
    for task in (_project_tasks(h, wq_ref, q_s, N_CHUNK_PROJ) + _project_tasks(h, wk_ref, k_s, N_CHUNK_PROJ)
                 + _project_tasks(h, wv_ref, v_s, N_CHUNK_PROJ)
                 + _project_tasks(h, wg_ref, g_s, N_CHUNK_PROJ, silu=True)):
        task()
    fillers = (_project_tasks(h, wa_ref, qkv_ref, N_CHUNK_FILL) + _project_tasks(h, wza_ref, za_ref, N_CHUNK_FILL)
               + _project_tasks(h, wzr_ref, zr_ref, N_CHUNK_FILL))

    ii = lax.broadcasted_iota(jnp.int32, (RET_CHUNK, RET_CHUNK), 0)
    jj = lax.broadcasted_iota(jnp.int32, (RET_CHUNK, RET_CHUNK), 1)
    causal = ii >= jj

    def rotate(t, c, s):
        t1, t2 = t[:, :half], t[:, half:]
        return jnp.concatenate([t1 * c - t2 * s, t2 * c + t1 * s], axis=1)

    for ci in range(TM_PROJ // RET_CHUNK):
        rows = slice(ci * RET_CHUNK, (ci + 1) * RET_CHUNK)
        c = cos_ref[rows, :]
        s = sin_ref[rows, :]
        for hd in range(n_heads):
            qk_cols = slice(hd * RET_QK_DIM, (hd + 1) * RET_QK_DIM)
            v_cols = slice(hd * RET_V_DIM, (hd + 1) * RET_V_DIM)
            q = q_s[rows, qk_cols].astype(F32)
            k = k_s[rows, qk_cols].astype(F32)
            v = v_s[rows, v_cols]
            qx = (rotate(q, c, s) * xi_ref[hd]).astype(BF16)
            kinv = (rotate(k, c, s) * ginv_ref[hd]).astype(BF16)
            inner = lax.dot_general(qx, kinv, (((1,), (1,)), ((), ())), preferred_element_type=F32)
            inner = jnp.where(causal, inner, 0.0).astype(BF16)
            state = state_ref[hd]
            lhs = jnp.concatenate([inner, qx], axis=1)
            rhs = jnp.concatenate([v, state.astype(BF16)], axis=0)
            out = jnp.dot(lhs, rhs, preferred_element_type=F32)
            upd = lax.dot_general(kinv, v, (((0,), (0,)), ((), ())), preferred_element_type=F32)
            state_ref[hd] = gchunk_ref[hd] * (state + upd)
            normed = out * lax.rsqrt(jnp.mean(out * out, axis=-1, keepdims=True) + EPS)
            ret_ref[rows, v_cols] = (g_s[rows, v_cols].astype(F32) * normed).astype(ret_ref.dtype)
            if fillers:
                fillers.pop(0)()
    for task in fillers:
        task()


def _proj_retention(x2d, gain, w_qr, w_kr, w_vr, w_gr, w_attn, w_za, w_zr, batch, seq):
    t, d = x2d.shape
    n_heads = w_qr.shape[1] // RET_QK_DIM
    steps = seq // TM_PROJ
    half = RET_QK_DIM // 2
    pos = jnp.arange(seq, dtype=F32)
    theta = 1.0 / (RET_ROT_BASE ** jnp.linspace(0.0, 1.0, half, dtype=F32))
    ang = pos[:, None] * theta[None, :]
    cos, sin = jnp.cos(ang), jnp.sin(ang)
    log_gamma = jnp.log(1.0 - 2.0 ** (-5.0 - jnp.arange(n_heads, dtype=F32)))
    i = jnp.arange(RET_CHUNK, dtype=F32)
    xi = jnp.exp((i + 1.0)[None, :] * log_gamma[:, None])[:, :, None]
    ginv = (jnp.exp(-(i + 1.0)[None, :] * log_gamma[:, None]) * (RET_QK_DIM ** -0.5))[:, :, None]
    gchunk = jnp.exp(RET_CHUNK * log_gamma)

    weights = [w_qr, w_kr, w_vr, w_gr, w_attn, w_za, w_zr]
    tok = lambda w: pl.BlockSpec((TM_PROJ, w), lambda b, n: (b * steps + n, 0))
    tab = pl.BlockSpec((TM_PROJ, half), lambda b, n: (n, 0))
    out_widths = [w_attn.shape[1], w_za.shape[1], w_zr.shape[1], w_vr.shape[1]]
    return pl.pallas_call(
        _proj_ret_kernel,
        grid=(batch, steps),
        in_specs=[tok(d), _resident((1, d))] + [_resident(w.shape) for w in weights]
        + [tab, tab, _resident(xi.shape), _resident(ginv.shape), pl.BlockSpec(memory_space=pltpu.SMEM)],
        out_specs=[tok(w) for w in out_widths],
        out_shape=[jax.ShapeDtypeStruct((t, w), BF16) for w in out_widths],
        scratch_shapes=[pltpu.VMEM((n_heads, RET_QK_DIM, RET_V_DIM), F32)]
        + [pltpu.VMEM((TM_PROJ, w.shape[1]), BF16) for w in (w_qr, w_kr, w_vr, w_gr)],
        compiler_params=pltpu.CompilerParams(
            dimension_semantics=("parallel", "arbitrary"), vmem_limit_bytes=VMEM_LIMIT_BYTES),
        name="proj_retention",
    )(x2d, gain, *weights, cos, sin, xi, ginv, gchunk)


def _attn_kernel(sink_ref, q_ref, kv_ref, kvp_ref, kg_ref, bd_ref, indt_ref, o_ref):
    n = pl.program_id(1)
    nsub = TS_ATTN // WINDOW
    n_pairs = q_ref.shape[1] // LANES
    pairs_per_group = GROUP // 2
    n_groups = n_pairs // pairs_per_group

    kv_all = jnp.concatenate([kvp_ref[...], kv_ref[...]], axis=0).astype(F32)
    k_all = kv_all[:, :LANES]
    v_all = kv_all[:, LANES:]
    k_ms = jnp.dot((k_all * k_all).astype(BF16), bd_ref[...], preferred_element_type=F32) * (1.0 / HEAD_DIM)
    k_all = k_all * lax.rsqrt(k_ms + EPS) * kg_ref[...]
    lane = lax.broadcasted_iota(jnp.int32, k_all.shape, 1)
    lo = lane < HEAD_DIM
    k_sw = pltpu.roll(k_all, HEAD_DIM, axis=1)
    zero = jnp.zeros_like(k_all)
    k_top = [jnp.where(lo, k_all, zero).astype(BF16), jnp.where(lo, k_sw, zero).astype(BF16)]
    k_bot = [jnp.where(lo, zero, k_sw).astype(BF16), jnp.where(lo, zero, k_all).astype(BF16)]
    v_t = v_all.T
    ones_rows = jnp.ones((SUM_ROWS, v_t.shape[1]), F32)
    vt_ext = [jnp.concatenate([v_t[g * HEAD_DIM:(g + 1) * HEAD_DIM], ones_rows], axis=0).astype(BF16)
              for g in range(n_groups)]

    key = lax.broadcasted_iota(jnp.int32, (WINDOW, WINDOW), 0)
    qry = lax.broadcasted_iota(jnp.int32, (WINDOW, WINDOW), 1)
    upper = key > qry
    upper_bf = upper.astype(F32).astype(BF16)
    prev_pen = jnp.where(n == 0, NEG_BIG, 0.0).astype(F32)

    qnorm = {}

    def scores(j, g):
        q = q_ref[j * WINDOW:(j + 1) * WINDOW, :]
        if j not in qnorm:
            ssq_t = lax.dot_general(indt_ref[...], q * q, (((1,), (1,)), ((), ())), preferred_element_type=F32)
            qnorm[j] = lax.rsqrt(ssq_t * (1.0 / HEAD_DIM) + EPS)
        p0 = g * pairs_per_group
        r0 = j * WINDOW
        qs = jnp.concatenate(
            [q[:, (p0 + pp) * LANES:(p0 + pp + 1) * LANES] for pp in range(pairs_per_group)], axis=0)
        kb = jnp.concatenate([k_top[g][r0:r0 + 2 * WINDOW], k_bot[g][r0:r0 + 2 * WINDOW]], axis=0)
        s_t = lax.dot_general(kb, qs, (((1,), (1,)), ((), ())), preferred_element_type=F32)
        return s_t, qnorm[j]

    def softmax_t(j, g, s_t, r_t):
        p0 = g * pairs_per_group
        prob_cols, sink_cols = [], []
        for pp in range(pairs_per_group):
            cs = slice(pp * WINDOW, (pp + 1) * WINDOW)
            for e in range(2):
                sp = s_t[e * 2 * WINDOW:e * 2 * WINDOW + WINDOW, cs]
                sc = s_t[e * 2 * WINDOW + WINDOW:(e + 1) * 2 * WINDOW, cs]
                if j == 0:
                    sp = sp + prev_pen
                comb = jnp.where(upper, sp, sc)
                hd = 2 * (p0 + pp) + e
                sink = sink_ref[hd]
                r_row = r_t[hd:hd + 1, :]
                m = jnp.maximum(jnp.max(comb, axis=0, keepdims=True) * r_row, sink)
                p = jnp.exp2(comb * r_row - m).astype(BF16)
                p_prev = p * upper_bf
                prob_cols.append(jnp.concatenate([p_prev, p - p_prev], axis=0))
                sink_cols.append(jnp.exp2(sink - m))
        return jnp.concatenate(prob_cols, axis=1), jnp.concatenate(sink_cols, axis=1)

    def outputs(j, g, p_t, sink_t):
        p0 = g * pairs_per_group
        r0 = j * WINDOW
        o_t = jnp.dot(vt_ext[g][:, r0:r0 + 2 * WINDOW], p_t, preferred_element_type=F32)
        inv = 1.0 / (o_t[HEAD_DIM:HEAD_DIM + 1, :] + sink_t)
        on = o_t[:HEAD_DIM, :] * inv
        for pp in range(pairs_per_group):
            pair = jnp.concatenate([on[:, (2 * pp) * WINDOW:(2 * pp + 1) * WINDOW],
                                    on[:, (2 * pp + 1) * WINDOW:(2 * pp + 2) * WINDOW]], axis=0)
            cols = slice((p0 + pp) * LANES, (p0 + pp + 1) * LANES)
            o_ref[j * WINDOW:(j + 1) * WINDOW, cols] = pair.T.astype(o_ref.dtype)

    units = [(j, g) for j in range(nsub) for g in range(n_groups)]
    pending = [scores(*u) for u in units[:SCORE_LOOKAHEAD]]
    probs = None
    for idx, unit in enumerate(units):
        if idx + SCORE_LOOKAHEAD < len(units):
            pending.append(scores(*units[idx + SCORE_LOOKAHEAD]))
        nxt = softmax_t(*unit, *pending.pop(0))
        if probs is not None:
            outputs(*units[idx - 1], *probs)
        probs = nxt
    outputs(*units[-1], *probs)


def _attention(qkv, sinks, k_gain_row, batch, seq, d_q):
    t = qkv.shape[0]
    steps = seq // TS_ATTN
    sub = TS_ATTN // WINDOW
    kv_blk = d_q // (2 * LANES)
    ii = np.arange(LANES)
    dd = np.arange(d_q)
    hh = np.arange(d_q // HEAD_DIM)
    bd = jnp.asarray((ii[:, None] // HEAD_DIM) == (ii[None, :] // HEAD_DIM), BF16)
    ind_t = jnp.asarray(hh[:, None] == (dd[None, :] // HEAD_DIM), BF16)
    return pl.pallas_call(
        _attn_kernel,
        grid=(batch, steps),
        in_specs=[
            pl.BlockSpec(memory_space=pltpu.SMEM),
            pl.BlockSpec((TS_ATTN, d_q), lambda b, n: (b * steps + n, 0)),
            pl.BlockSpec((TS_ATTN, 2 * LANES), lambda b, n: (b * steps + n, kv_blk)),
            pl.BlockSpec((WINDOW, 2 * LANES),
                         lambda b, n: ((b * steps + n) * sub - jnp.minimum(n, 1), kv_blk)),
            _resident((1, LANES)),
            _resident((LANES, LANES)),
            _resident(ind_t.shape),
        ],
        out_specs=pl.BlockSpec((TS_ATTN, d_q), lambda b, n: (b * steps + n, 0)),
        out_shape=jax.ShapeDtypeStruct((t, d_q), BF16),
        compiler_params=pltpu.CompilerParams(
            dimension_semantics=("parallel", "parallel"), vmem_limit_bytes=VMEM_LIMIT_BYTES),
        name="swa_attention",
    )(sinks, qkv, qkv, qkv, k_gain_row, bd, ind_t)


def _out_kernel(x_ref, attn_ref, ret_ref, za_ref, zr_ref, wba_ref, wbr_ref, wo_ref, gain_ref,
                wg_ref, wu_ref, wd_ref, o_ref):
    ba = jnp.dot(attn_ref[...], wba_ref[...], preferred_element_type=F32)
    br = jnp.dot(ret_ref[...], wbr_ref[...], preferred_element_type=F32)
    merged = (jax.nn.sigmoid(za_ref[...].astype(F32)) * ba
              + jax.nn.sigmoid(zr_ref[...].astype(F32)) * br)
    x1 = x_ref[...] + jnp.dot(merged.astype(BF16), wo_ref[...], preferred_element_type=F32)
    ms = jnp.mean(x1 * x1, axis=-1, keepdims=True)
    h = (x1 * lax.rsqrt(ms + EPS) * gain_ref[...]).astype(BF16)
    acc = x1
    d_ff = wg_ref.shape[1]
    for c in range(0, d_ff, FF_CHUNK):
        gate = jnp.dot(h, wg_ref[:, c:c + FF_CHUNK], preferred_element_type=F32)
        up = jnp.dot(h, wu_ref[:, c:c + FF_CHUNK], preferred_element_type=F32)
        act = (gate * jax.nn.sigmoid(gate) * up).astype(BF16)
        acc = acc + jnp.dot(act, wd_ref[c:c + FF_CHUNK, :], preferred_element_type=F32)
    o_ref[...] = acc


def _merge_ffn(x2d, attn, ret, z_a, z_r, wba, wbr, wo, gain, wg, wu, wd):
    t, d = x2d.shape
    tok = lambda w: pl.BlockSpec((TM_OUT, w), lambda i: (i, 0))
    return pl.pallas_call(
        _out_kernel,
        grid=(t // TM_OUT,),
        in_specs=[tok(d), tok(attn.shape[1]), tok(ret.shape[1]), tok(d), tok(d),
                  _resident(wba.shape), _resident(wbr.shape), _resident(wo.shape),
                  _resident(gain.shape), _resident(wg.shape), _resident(wu.shape),
                  _resident(wd.shape)],
        out_specs=tok(d),
        out_shape=jax.ShapeDtypeStruct((t, d), F32),
        compiler_params=pltpu.CompilerParams(
            dimension_semantics=("parallel",), vmem_limit_bytes=VMEM_LIMIT_BYTES),
        name="merge_ffn",
    )(x2d, attn, ret, z_a, z_r, wba, wbr, wo, gain, wg, wu, wd)


def _deinterleave_heads(w, n_heads, head_dim):
    d = w.shape[0]
    w = w.reshape(d, n_heads, head_dim // 2, 2)
    return jnp.swapaxes(w, 2, 3).reshape(d, n_heads * head_dim)


def kernel(x, norm_mix_gain, w_in, q_norm_gain, k_norm_gain, attn_sinks, w_branch_attn, w_branch_ret,
           w_out, norm_ffn_gain, w_ffn_gate, w_ffn_up, w_ffn_down):
    batch, seq, d = x.shape
    depth = w_in.shape[0]
    att_q = w_branch_attn.shape[1]
    ret_v = w_branch_ret.shape[1]
    n_q_heads = att_q // HEAD_DIM
    att_kv = (n_q_heads // GROUP) * HEAD_DIM
    ret_heads = ret_v // RET_V_DIM
    ret_qk = ret_heads * RET_QK_DIM
    splits = np.cumsum([att_q, att_kv, att_kv, ret_qk, ret_qk, ret_v, ret_v, d])
    assert att_kv == LANES and w_in.shape[2] == splits[-1] + d

    x2d = x.reshape(batch * seq, d)
    for l in range(depth):
        wq_a, wk_a, wv_a, wq_r, wk_r, wv_r, wg_r, wz_a, wz_r = jnp.split(w_in[l], splits, axis=1)
        weights = [
            _deinterleave_heads(wq_r, ret_heads, RET_QK_DIM),
            _deinterleave_heads(wk_r, ret_heads, RET_QK_DIM),
            wv_r, wg_r,
            jnp.concatenate([wq_a, wk_a, wv_a], axis=1),
            wz_a, wz_r,
        ]
        weights = [w.astype(BF16) for w in weights]
        qkv_a, z_a, z_r, ret = _proj_retention(x2d, norm_mix_gain[l][None, :], *weights, batch, seq)

        k_gain_row = jnp.tile(k_norm_gain[l] * q_norm_gain[l] * (HEAD_DIM ** -0.5 * LOG2E),
                              LANES // HEAD_DIM)[None, :]
        attn = _attention(qkv_a, attn_sinks[l] * LOG2E, k_gain_row, batch, seq, att_q)

        x2d = _merge_ffn(
            x2d, attn, ret, z_a, z_r,
            w_branch_attn[l].astype(BF16), w_branch_ret[l].astype(BF16), w_out[l].astype(BF16),
            norm_ffn_gain[l][None, :],
            w_ffn_gate[l].astype(BF16), w_ffn_up[l].astype(BF16), w_ffn_down[l].astype(BF16))
    return x2d.reshape(batch, seq, d)
```

```python
import functools
import math

import jax
import jax.numpy as jnp
import numpy as np
from jax import lax
from jax.experimental import pallas as pl
from jax.experimental.pallas import tpu as pltpu

F32 = jnp.float32
BF16 = jnp.bfloat16

EPS = 1e-6
HEAD_DIM = 64
GROUP = 8
WINDOW = 128
RET_QK_DIM = 256
RET_V_DIM = 512
RET_CHUNK = 256
RET_ROT_BASE = 10000.0

VMEM_LIMIT_BYTES = 56 * 1024 * 1024
LANES = 128

TM_PROJ = 512
N_CHUNK_PROJ = 512
N_CHUNK_FILL = 256
TS_ATTN = 2048
TM_OUT = 512
FF_CHUNK = 256

NEG_BIG = -1e30
SUM_ROWS = 16
SCORE_LOOKAHEAD = 2
LOG2E = math.log2(math.e)


def _resident(shape):
    nd = len(shape)
    return pl.BlockSpec(shape, lambda *_: (0,) * nd, pipeline_mode=pl.Buffered(1))


def _project_tasks(h, w_ref, o_ref, chunk, silu=False):
    def task(c, cw):
        y = jnp.dot(h, w_ref[:, c:c + cw], preferred_element_type=F32)
        if silu:
            y = y * jax.nn.sigmoid(y)
        o_ref[:, c:c + cw] = y.astype(BF16)

    width = w_ref.shape[1]
    return [functools.partial(task, c, min(chunk, width - c)) for c in range(0, width, chunk)]


def _proj_ret_kernel(x_ref, wq_ref, wk_ref, wv_ref, wg_ref, wa_ref, wza_ref, wzr_ref,
                     cos_ref, sin_ref, xi_ref, ginv_ref, gchunk_ref,
                     qkv_ref, za_ref, zr_ref, ret_ref,
                     state_ref, q_s, k_s, v_s, g_s):
    n_heads = state_ref.shape[0]
    half = RET_QK_DIM // 2

    @pl.when(pl.program_id(1) == 0)
    def _():
        state_ref[...] = jnp.zeros_like(state_ref)

    x = x_ref[...]
    h = (x * lax.rsqrt(jnp.mean(x * x, axis=-1, keepdims=True) + EPS)).astype(BF16)
    for task in (_project_tasks(h, wq_ref, q_s, N_CHUNK_PROJ) + _project_tasks(h, wk_ref, k_s, N_CHUNK_PROJ)
                 + _project_tasks(h, wv_ref, v_s, N_CHUNK_PROJ)
                 + _project_tasks(h, wg_ref, g_s, N_CHUNK_PROJ, silu=True)):
        task()
    fillers = (_project_tasks(h, wa_ref, qkv_ref, N_CHUNK_FILL) + _project_tasks(h, wza_ref, za_ref, N_CHUNK_FILL)
               + _project_tasks(h, wzr_ref, zr_ref, N_CHUNK_FILL))

    ii = lax.broadcasted_iota(jnp.int32, (RET_CHUNK, RET_CHUNK), 0)
    jj = lax.broadcasted_iota(jnp.int32, (RET_CHUNK, RET_CHUNK), 1)
    causal = ii >= jj

    def rotate(t, c, s):
        t1, t2 = t[:, :half], t[:, half:]
        return jnp.concatenate([t1 * c - t2 * s, t2 * c + t1 * s], axis=1)

    def rows_of(ci):
        return slice(ci * RET_CHUNK, (ci + 1) * RET_CHUNK)

    def intra(ci, hd):
        rows = rows_of(ci)
        qk_cols = slice(hd * RET_QK_DIM, (hd + 1) * RET_QK_DIM)
        c = cos_ref[rows, :]
        s = sin_ref[rows, :]
        q = q_s[rows, qk_cols].astype(F32)
        k = k_s[rows, qk_cols].astype(F32)
        qx = (rotate(q, c, s) * xi_ref[hd]).astype(BF16)
        kinv = (rotate(k, c, s) * ginv_ref[hd]).astype(BF16)
        inner = lax.dot_general(qx, kinv, (((1,), (1,)), ((), ())), preferred_element_type=F32)
        return qx, kinv, jnp.where(causal, inner, 0.0).astype(BF16)

    def recur(ci, hd, qx, kinv, inner):
        v = v_s[rows_of(ci), hd * RET_V_DIM:(hd + 1) * RET_V_DIM]
        state = state_ref[hd]
        lhs = jnp.concatenate([inner, qx], axis=1)
        rhs = jnp.concatenate([v, state.astype(BF16)], axis=0)
        out = jnp.dot(lhs, rhs, preferred_element_type=F32)
        upd = lax.dot_general(kinv, v, (((0,), (0,)), ((), ())), preferred_element_type=F32)
        state_ref[hd] = gchunk_ref[hd] * (state + upd)
        return out

    def emit(ci, hd, out):
        rows = rows_of(ci)
        v_cols = slice(hd * RET_V_DIM, (hd + 1) * RET_V_DIM)
        normed = out * lax.rsqrt(jnp.mean(out * out, axis=-1, keepdims=True) + EPS)
        ret_ref[rows, v_cols] = (g_s[rows, v_cols].astype(F32) * normed).astype(ret_ref.dtype)

    units = [(ci, hd) for ci in range(TM_PROJ // RET_CHUNK) for hd in range(n_heads)]
    staged = intra(*units[0])
    out_prev = None
    for idx, unit in enumerate(units):
        current = staged
        if idx + 1 < len(units):
            staged = intra(*units[idx + 1])
        out = recur(*unit, *current)
        if out_prev is not None:
            emit(*units[idx - 1], out_prev)
        out_prev = out
        if fillers:
            fillers.pop(0)()
    emit(*units[-1], out_prev)
    for task in fillers:
        task()


def _proj_retention(x2d, w_qr, w_kr, w_vr, w_gr, w_attn, w_za, w_zr, batch, seq):
    t, d = x2d.shape
    n_heads = w_qr.shape[1] // RET_QK_DIM
    steps = seq // TM_PROJ
    half = RET_QK_DIM // 2
    pos = jnp.arange(seq, dtype=F32)
    theta = 1.0 / (RET_ROT_BASE ** jnp.linspace(0.0, 1.0, half, dtype=F32))
    ang = pos[:, None] * theta[None, :]
    cos, sin = jnp.cos(ang), jnp.sin(ang)
    log_gamma = jnp.log(1.0 - 2.0 ** (-5.0 - jnp.arange(n_heads, dtype=F32)))
    i = jnp.arange(RET_CHUNK, dtype=F32)
    xi = jnp.exp((i + 1.0)[None, :] * log_gamma[:, None])[:, :, None]
    ginv = (jnp.exp(-(i + 1.0)[None, :] * log_gamma[:, None]) * (RET_QK_DIM ** -0.5))[:, :, None]
    gchunk = jnp.exp(RET_CHUNK * log_gamma)

    weights = [w_qr, w_kr, w_vr, w_gr, w_attn, w_za, w_zr]
    tok = lambda w: pl.BlockSpec((TM_PROJ, w), lambda b, n: (b * steps + n, 0))
    tab = pl.BlockSpec((TM_PROJ, half), lambda b, n: (n, 0))
    out_widths = [w_attn.shape[1], w_za.shape[1], w_zr.shape[1], w_vr.shape[1]]
    return pl.pallas_call(
        _proj_ret_kernel,
        grid=(batch, steps),
        in_specs=[tok(d)] + [_resident(w.shape) for w in weights]
        + [tab, tab, _resident(xi.shape), _resident(ginv.shape), pl.BlockSpec(memory_space=pltpu.SMEM)],
        out_specs=[tok(w) for w in out_widths],
        out_shape=[jax.ShapeDtypeStruct((t, w), BF16) for w in out_widths],
        scratch_shapes=[pltpu.VMEM((n_heads, RET_QK_DIM, RET_V_DIM), F32)]
        + [pltpu.VMEM((TM_PROJ, w.shape[1]), BF16) for w in (w_qr, w_kr, w_vr, w_gr)],
        compiler_params=pltpu.CompilerParams(
            dimension_semantics=("parallel", "arbitrary"), vmem_limit_bytes=VMEM_LIMIT_BYTES),
        name="proj_retention",
    )(x2d, *weights, cos, sin, xi, ginv, gchunk)


def _attn_kernel(sink_ref, q_ref, kv_ref, kvp_ref, kg_ref, bd_ref, indt_ref, o_ref):
    n = pl.program_id(1)
    nsub = TS_ATTN // WINDOW
    n_pairs = q_ref.shape[1] // LANES
    pairs_per_group = GROUP // 2
    n_groups = n_pairs // pairs_per_group

    kv_all = jnp.concatenate([kvp_ref[...], kv_ref[...]], axis=0).astype(F32)
    k_all = kv_all[:, :LANES]
    v_all = kv_all[:, LANES:]
    k_ms = jnp.dot((k_all * k_all).astype(BF16), bd_ref[...], preferred_element_type=F32) * (1.0 / HEAD_DIM)
    k_all = k_all * lax.rsqrt(k_ms + EPS) * kg_ref[...]
    lane = lax.broadcasted_iota(jnp.int32, k_all.shape, 1)
    lo = lane < HEAD_DIM
    k_sw = pltpu.roll(k_all, HEAD_DIM, axis=1)
    zero = jnp.zeros_like(k_all)
    k_top = [jnp.where(lo, k_all, zero).astype(BF16), jnp.where(lo, k_sw, zero).astype(BF16)]
    k_bot = [jnp.where(lo, zero, k_sw).astype(BF16), jnp.where(lo, zero, k_all).astype(BF16)]
    v_t = v_all.T
    ones_rows = jnp.ones((SUM_ROWS, v_t.shape[1]), F32)
    vt_ext = [jnp.concatenate([v_t[g * HEAD_DIM:(g + 1) * HEAD_DIM], ones_rows], axis=0).astype(BF16)
              for g in range(n_groups)]

    key = lax.broadcasted_iota(jnp.int32, (WINDOW, WINDOW), 0)
    qry = lax.broadcasted_iota(jnp.int32, (WINDOW, WINDOW), 1)
    upper = key > qry
    upper_bf = upper.astype(F32).astype(BF16)
    prev_pen = jnp.where(n == 0, NEG_BIG, 0.0).astype(F32)

    qnorm = {}

    def scores(j, g):
        q = q_ref[j * WINDOW:(j + 1) * WINDOW, :]
        if j not in qnorm:
            ssq_t = lax.dot_general(indt_ref[...], q * q, (((1,), (1,)), ((), ())), preferred_element_type=F32)
            qnorm[j] = lax.rsqrt(ssq_t * (1.0 / HEAD_DIM) + EPS)
        p0 = g * pairs_per_group
        r0 = j * WINDOW
        qs = jnp.concatenate(
            [q[:, (p0 + pp) * LANES:(p0 + pp + 1) * LANES] for pp in range(pairs_per_group)], axis=0)
        kb = jnp.concatenate([k_top[g][r0:r0 + 2 * WINDOW], k_bot[g][r0:r0 + 2 * WINDOW]], axis=0)
        s_t = lax.dot_general(kb, qs, (((1,), (1,)), ((), ())), preferred_element_type=F32)
        return s_t, qnorm[j]

    def softmax_t(j, g, s_t, r_t):
        p0 = g * pairs_per_group
        prob_cols, sink_cols = [], []
        for pp in range(pairs_per_group):
            cs = slice(pp * WINDOW, (pp + 1) * WINDOW)
            for e in range(2):
                sp = s_t[e * 2 * WINDOW:e * 2 * WINDOW + WINDOW, cs]
                sc = s_t[e * 2 * WINDOW + WINDOW:(e + 1) * 2 * WINDOW, cs]
                if j == 0:
                    sp = sp + prev_pen
                comb = jnp.where(upper, sp, sc)
                hd = 2 * (p0 + pp) + e
                sink = sink_ref[hd]
                r_row = r_t[hd:hd + 1, :]
                m = jnp.maximum(jnp.max(comb, axis=0, keepdims=True) * r_row, sink)
                p = jnp.exp2(comb * r_row - m).astype(BF16)
                p_prev = p * upper_bf
                prob_cols.append(jnp.concatenate([p_prev, p - p_prev], axis=0))
                sink_cols.append(jnp.exp2(sink - m))
        return jnp.concatenate(prob_cols, axis=1), jnp.concatenate(sink_cols, axis=1)

    def outputs(j, g, p_t, sink_t):
        p0 = g * pairs_per_group
        r0 = j * WINDOW
        o_t = jnp.dot(vt_ext[g][:, r0:r0 + 2 * WINDOW], p_t, preferred_element_type=F32)
        inv = 1.0 / (o_t[HEAD_DIM:HEAD_DIM + 1, :] + sink_t)
        on = o_t[:HEAD_DIM, :] * inv
        for pp in range(pairs_per_group):
            pair = jnp.concatenate([on[:, (2 * pp) * WINDOW:(2 * pp + 1) * WINDOW],
                                    on[:, (2 * pp + 1) * WINDOW:(2 * pp + 2) * WINDOW]], axis=0)
            cols = slice((p0 + pp) * LANES, (p0 + pp + 1) * LANES)
            o_ref[j * WINDOW:(j + 1) * WINDOW, cols] = pair.T.astype(o_ref.dtype)

    units = [(j, g) for j in range(nsub) for g in range(n_groups)]
    pending = [scores(*u) for u in units[:SCORE_LOOKAHEAD]]
    probs = None
    for idx, unit in enumerate(units):
        if idx + SCORE_LOOKAHEAD < len(units):
            pending.append(scores(*units[idx + SCORE_LOOKAHEAD]))
        nxt = softmax_t(*unit, *pending.pop(0))
        if probs is not None:
            outputs(*units[idx - 1], *probs)
        probs = nxt
    outputs(*units[-1], *probs)


def _attention(qkv, sinks, k_gain_row, batch, seq, d_q):
    t = qkv.shape[0]
    steps = seq // TS_ATTN
    sub = TS_ATTN // WINDOW
    kv_blk = d_q // (2 * LANES)
    ii = np.arange(LANES)
    dd = np.arange(d_q)
    hh = np.arange(d_q // HEAD_DIM)
    bd = jnp.asarray((ii[:, None] // HEAD_DIM) == (ii[None, :] // HEAD_DIM), BF16)
    ind_t = jnp.asarray(hh[:, None] == (dd[None, :] // HEAD_DIM), BF16)
    return pl.pallas_call(
        _attn_kernel,
        grid=(batch, steps),
        in_specs=[
            pl.BlockSpec(memory_space=pltpu.SMEM),
            pl.BlockSpec((TS_ATTN, d_q), lambda b, n: (b * steps + n, 0)),
            pl.BlockSpec((TS_ATTN, 2 * LANES), lambda b, n: (b * steps + n, kv_blk)),
            pl.BlockSpec((WINDOW, 2 * LANES),
                         lambda b, n: ((b * steps + n) * sub - jnp.minimum(n, 1), kv_blk)),
            _resident((1, LANES)),
            _resident((LANES, LANES)),
            _resident(ind_t.shape),
        ],
        out_specs=pl.BlockSpec((TS_ATTN, d_q), lambda b, n: (b * steps + n, 0)),
        out_shape=jax.ShapeDtypeStruct((t, d_q), BF16),
        compiler_params=pltpu.CompilerParams(
            dimension_semantics=("parallel", "parallel"), vmem_limit_bytes=VMEM_LIMIT_BYTES),
        name="swa_attention",
    )(sinks, qkv, qkv, qkv, k_gain_row, bd, ind_t)


def _out_kernel(x_ref, attn_ref, ret_ref, za_ref, zr_ref, wba_ref, wbr_ref, wo_ref,
                wg_ref, wu_ref, wd_ref, o_ref):
    ba = jnp.dot(attn_ref[...], wba_ref[...], preferred_element_type=F32)
    br = jnp.dot(ret_ref[...], wbr_ref[...], preferred_element_type=F32)
    merged = (jax.nn.sigmoid(za_ref[...].astype(F32)) * ba
              + jax.nn.sigmoid(zr_ref[...].astype(F32)) * br)
    x1 = x_ref[...] + jnp.dot(merged.astype(BF16), wo_ref[...], preferred_element_type=F32)
    h = (x1 * lax.rsqrt(jnp.mean(x1 * x1, axis=-1, keepdims=True) + EPS)).astype(BF16)
    acc = x1
    d_ff = wg_ref.shape[1]
    for c in range(0, d_ff, FF_CHUNK):
        gate = jnp.dot(h, wg_ref[:, c:c + FF_CHUNK], preferred_element_type=F32)
        up = jnp.dot(h, wu_ref[:, c:c + FF_CHUNK], preferred_element_type=F32)
        act = (gate * jax.nn.sigmoid(gate) * up).astype(BF16)
        acc = acc + jnp.dot(act, wd_ref[c:c + FF_CHUNK, :], preferred_element_type=F32)
    o_ref[...] = acc


def _merge_ffn(x2d, attn, ret, z_a, z_r, wba, wbr, wo, wg, wu, wd):
    t, d = x2d.shape
    tok = lambda w: pl.BlockSpec((TM_OUT, w), lambda i: (i, 0))
    return pl.pallas_call(
        _out_kernel,
        grid=(t // TM_OUT,),
        in_specs=[tok(d), tok(attn.shape[1]), tok(ret.shape[1]), tok(d), tok(d),
                  _resident(wba.shape), _resident(wbr.shape), _resident(wo.shape),
                  _resident(wg.shape), _resident(wu.shape),
                  _resident(wd.shape)],
        out_specs=tok(d),
        out_shape=jax.ShapeDtypeStruct((t, d), F32),
        compiler_params=pltpu.CompilerParams(
            dimension_semantics=("parallel",), vmem_limit_bytes=VMEM_LIMIT_BYTES),
        name="merge_ffn",
    )(x2d, attn, ret, z_a, z_r, wba, wbr, wo, wg, wu, wd)


def _deinterleave_heads(w, n_heads, head_dim):
    d = w.shape[0]
    w = w.reshape(d, n_heads, head_dim // 2, 2)
    return jnp.swapaxes(w, 2, 3).reshape(d, n_heads * head_dim)


def kernel(x, norm_mix_gain, w_in, q_norm_gain, k_norm_gain, attn_sinks, w_branch_attn, w_branch_ret,
           w_out, norm_ffn_gain, w_ffn_gate, w_ffn_up, w_ffn_down):
    batch, seq, d = x.shape
    depth = w_in.shape[0]
    att_q = w_branch_attn.shape[1]
    ret_v = w_branch_ret.shape[1]
    n_q_heads = att_q // HEAD_DIM
    att_kv = (n_q_heads // GROUP) * HEAD_DIM
    ret_heads = ret_v // RET_V_DIM
    ret_qk = ret_heads * RET_QK_DIM
    splits = np.cumsum([att_q, att_kv, att_kv, ret_qk, ret_qk, ret_v, ret_v, d])
    assert att_kv == LANES and w_in.shape[2] == splits[-1] + d

    x2d = x.reshape(batch * seq, d)
    for l in range(depth):
        wq_a, wk_a, wv_a, wq_r, wk_r, wv_r, wg_r, wz_a, wz_r = jnp.split(w_in[l], splits, axis=1)
        weights = [
            _deinterleave_heads(wq_r, ret_heads, RET_QK_DIM),
            _deinterleave_heads(wk_r, ret_heads, RET_QK_DIM),
            wv_r, wg_r,
            jnp.concatenate([wq_a, wk_a, wv_a], axis=1),
            wz_a, wz_r,
        ]
        weights = [(norm_mix_gain[l][:, None] * w).astype(BF16) for w in weights]
        qkv_a, z_a, z_r, ret = _proj_retention(x2d, *weights, batch, seq)

        k_gain_row = jnp.tile(k_norm_gain[l] * q_norm_gain[l] * (HEAD_DIM ** -0.5 * LOG2E),
                              LANES // HEAD_DIM)[None, :]
        attn = _attention(qkv_a, attn_sinks[l] * LOG2E, k_gain_row, batch, seq, att_q)

        x2d = _merge_ffn(
            x2d, attn, ret, z_a, z_r,
            w_branch_attn[l].astype(BF16), w_branch_ret[l].astype(BF16), w_out[l].astype(BF16),
            (norm_ffn_gain[l][:, None] * w_ffn_gate[l]).astype(BF16),
            (norm_ffn_gain[l][:, None] * w_ffn_up[l]).astype(BF16), w_ffn_down[l].astype(BF16))
    return x2d.reshape(batch, seq, d)
```

```python
import functools
import math

import jax
import jax.numpy as jnp
import numpy as np
from jax import lax
from jax.experimental import pallas as pl
from jax.experimental.pallas import tpu as pltpu

F32 = jnp.float32
BF16 = jnp.bfloat16

EPS = 1e-6
HEAD_DIM = 64
GROUP = 8
WINDOW = 128
RET_QK_DIM = 256
RET_V_DIM = 512
RET_CHUNK = 256
RET_ROT_BASE = 10000.0

VMEM_LIMIT_BYTES = 56 * 1024 * 1024
LANES = 128

TM_PROJ = 512
N_CHUNK_PROJ = 512
N_CHUNK_FILL = 256
TS_ATTN = 2048
TM_OUT = 512
FF_CHUNK = 256

NEG_BIG = -1e30
SUM_ROWS = 16
SCORE_LOOKAHEAD = 2
LOG2E = math.log2(math.e)


def _resident(shape):
    nd = len(shape)
    return pl.BlockSpec(shape, lambda *_: (0,) * nd, pipeline_mode=pl.Buffered(1))


def _project_tasks(h, w_ref, o_ref, chunk, silu=False):
    def task(c, cw):
        y = jnp.dot(h, w_ref[:, c:c + cw], preferred_element_type=F32)
        if silu:
            y = y * jax.nn.sigmoid(y)
        o_ref[:, c:c + cw] = y.astype(BF16)

    width = w_ref.shape[1]
    return [functools.partial(task, c, min(chunk, width - c)) for c in range(0, width, chunk)]


def _proj_ret_kernel(x_ref, wq_ref, wk_ref, wv_ref, wg_ref, wa_ref, wza_ref, wzr_ref,
                     cos_ref, sin_ref, xi_ref, ginv_ref, gchunk_ref,
                     qkv_ref, za_ref, zr_ref, ret_ref,
                     state_ref, q_s, k_s, v_s, g_s):
    n_heads = state_ref.shape[0]
    half = RET_QK_DIM // 2

    @pl.when(pl.program_id(1) == 0)
    def _():
        state_ref[...] = jnp.zeros_like(state_ref)

    x = x_ref[...]
    h = (x * lax.rsqrt(jnp.mean(x * x, axis=-1, keepdims=True) + EPS)).astype(BF16)
    for task in (_project_tasks(h, wq_ref, q_s, N_CHUNK_PROJ) + _project_tasks(h, wk_ref, k_s, N_CHUNK_PROJ)
                 + _project_tasks(h, wv_ref, v_s, N_CHUNK_PROJ)
                 + _project_tasks(h, wg_ref, g_s, N_CHUNK_PROJ, silu=True)):
        task()
    fillers = (_project_tasks(h, wa_ref, qkv_ref, N_CHUNK_FILL) + _project_tasks(h, wza_ref, za_ref, N_CHUNK_FILL)
               + _project_tasks(h, wzr_ref, zr_ref, N_CHUNK_FILL))

    ii = lax.broadcasted_iota(jnp.int32, (RET_CHUNK, RET_CHUNK), 0)
    jj = lax.broadcasted_iota(jnp.int32, (RET_CHUNK, RET_CHUNK), 1)
    causal = ii >= jj

    def rotate(t, c, s):
        t1, t2 = t[:, :half], t[:, half:]
        return jnp.concatenate([t1 * c - t2 * s, t2 * c + t1 * s], axis=1)

    def rows_of(ci):
        return slice(ci * RET_CHUNK, (ci + 1) * RET_CHUNK)

    def intra(ci, hd):
        rows = rows_of(ci)
        qk_cols = slice(hd * RET_QK_DIM, (hd + 1) * RET_QK_DIM)
        c = cos_ref[rows, :]
        s = sin_ref[rows, :]
        q = q_s[rows, qk_cols].astype(F32)
        k = k_s[rows, qk_cols].astype(F32)
        qx = (rotate(q, c, s) * xi_ref[hd]).astype(BF16)
        kinv = (rotate(k, c, s) * ginv_ref[hd]).astype(BF16)
        inner = lax.dot_general(qx, kinv, (((1,), (1,)), ((), ())), preferred_element_type=F32)
        return qx, kinv, jnp.where(causal, inner, 0.0).astype(BF16)

    def recur(ci, hd, qx, kinv, inner):
        v = v_s[rows_of(ci), hd * RET_V_DIM:(hd + 1) * RET_V_DIM]
        state = state_ref[hd]
        lhs = jnp.concatenate([inner, qx], axis=1)
        rhs = jnp.concatenate([v, state.astype(BF16)], axis=0)
        out = jnp.dot(lhs, rhs, preferred_element_type=F32)
        upd = lax.dot_general(kinv, v, (((0,), (0,)), ((), ())), preferred_element_type=F32)
        state_ref[hd] = gchunk_ref[hd] * (state + upd)
        return out

    def emit(ci, hd, out):
        rows = rows_of(ci)
        v_cols = slice(hd * RET_V_DIM, (hd + 1) * RET_V_DIM)
        normed = out * lax.rsqrt(jnp.mean(out * out, axis=-1, keepdims=True) + EPS)
        ret_ref[rows, v_cols] = (g_s[rows, v_cols].astype(F32) * normed).astype(ret_ref.dtype)

    units = [(ci, hd) for ci in range(TM_PROJ // RET_CHUNK) for hd in range(n_heads)]
    staged = intra(*units[0])
    out_prev = None
    for idx, unit in enumerate(units):
        current = staged
        if idx + 1 < len(units):
            staged = intra(*units[idx + 1])
        out = recur(*unit, *current)
        if out_prev is not None:
            emit(*units[idx - 1], out_prev)
        out_prev = out
        if fillers:
            fillers.pop(0)()
    emit(*units[-1], out_prev)
    for task in fillers:
        task()


def _proj_retention(x2d, w_qr, w_kr, w_vr, w_gr, w_attn, w_za, w_zr, batch, seq):
    t, d = x2d.shape
    n_heads = w_qr.shape[1] // RET_QK_DIM
    steps = seq // TM_PROJ
    half = RET_QK_DIM // 2
    theta = 1.0 / (RET_ROT_BASE ** jnp.linspace(0.0, 1.0, half, dtype=F32))
    ang_hi = (jnp.arange(seq // LANES, dtype=F32) * LANES)[:, None] * theta[None, :]
    ang_lo = jnp.arange(LANES, dtype=F32)[:, None] * theta[None, :]
    c_hi, s_hi = jnp.cos(ang_hi)[:, None, :], jnp.sin(ang_hi)[:, None, :]
    c_lo, s_lo = jnp.cos(ang_lo)[None, :, :], jnp.sin(ang_lo)[None, :, :]
    cos = (c_hi * c_lo - s_hi * s_lo).reshape(seq, half)
    sin = (s_hi * c_lo + c_hi * s_lo).reshape(seq, half)
    log_gamma = jnp.log(1.0 - 2.0 ** (-5.0 - jnp.arange(n_heads, dtype=F32)))
    i = jnp.arange(RET_CHUNK, dtype=F32)
    xi = jnp.exp((i + 1.0)[None, :] * log_gamma[:, None])[:, :, None]
    ginv = (jnp.exp(-(i + 1.0)[None, :] * log_gamma[:, None]) * (RET_QK_DIM ** -0.5))[:, :, None]
    gchunk = jnp.exp(RET_CHUNK * log_gamma)

    weights = [w_qr, w_kr, w_vr, w_gr, w_attn, w_za, w_zr]
    tok = lambda w: pl.BlockSpec((TM_PROJ, w), lambda b, n: (b * steps + n, 0))
    tab = pl.BlockSpec((TM_PROJ, half), lambda b, n: (n, 0))
    out_widths = [w_attn.shape[1], w_za.shape[1], w_zr.shape[1], w_vr.shape[1]]
    return pl.pallas_call(
        _proj_ret_kernel,
        grid=(batch, steps),
        in_specs=[tok(d)] + [_resident(w.shape) for w in weights]
        + [tab, tab, _resident(xi.shape), _resident(ginv.shape), pl.BlockSpec(memory_space=pltpu.SMEM)],
        out_specs=[tok(w) for w in out_widths],
        out_shape=[jax.ShapeDtypeStruct((t, w), BF16) for w in out_widths],
        scratch_shapes=[pltpu.VMEM((n_heads, RET_QK_DIM, RET_V_DIM), F32)]
        + [pltpu.VMEM((TM_PROJ, w.shape[1]), BF16) for w in (w_qr, w_kr, w_vr, w_gr)],
        compiler_params=pltpu.CompilerParams(
            dimension_semantics=("parallel", "arbitrary"), vmem_limit_bytes=VMEM_LIMIT_BYTES),
        name="proj_retention",
    )(x2d, *weights, cos, sin, xi, ginv, gchunk)


def _attn_kernel(sink_ref, q_ref, kv_ref, kvp_ref, kg_ref, bd_ref, indt_ref, o_ref):
    n = pl.program_id(1)
    nsub = TS_ATTN // WINDOW
    n_pairs = q_ref.shape[1] // LANES
    pairs_per_group = GROUP // 2
    n_groups = n_pairs // pairs_per_group

    kv_all = jnp.concatenate([kvp_ref[...], kv_ref[...]], axis=0).astype(F32)
    k_all = kv_all[:, :LANES]
    v_all = kv_all[:, LANES:]
    k_ms = jnp.dot((k_all * k_all).astype(BF16), bd_ref[...], preferred_element_type=F32) * (1.0 / HEAD_DIM)
    k_all = k_all * lax.rsqrt(k_ms + EPS) * kg_ref[...]
    lane = lax.broadcasted_iota(jnp.int32, k_all.shape, 1)
    lo = lane < HEAD_DIM
    k_sw = pltpu.roll(k_all, HEAD_DIM, axis=1)
    zero = jnp.zeros_like(k_all)
    k_top = [jnp.where(lo, k_all, zero).astype(BF16), jnp.where(lo, k_sw, zero).astype(BF16)]
    k_bot = [jnp.where(lo, zero, k_sw).astype(BF16), jnp.where(lo, zero, k_all).astype(BF16)]
    v_t = v_all.T
    ones_rows = jnp.ones((SUM_ROWS, v_t.shape[1]), F32)
    vt_ext = [jnp.concatenate([v_t[g * HEAD_DIM:(g + 1) * HEAD_DIM], ones_rows], axis=0).astype(BF16)
              for g in range(n_groups)]

    key = lax.broadcasted_iota(jnp.int32, (WINDOW, WINDOW), 0)
    qry = lax.broadcasted_iota(jnp.int32, (WINDOW, WINDOW), 1)
    upper = key > qry
    upper_bf = upper.astype(F32).astype(BF16)
    prev_pen = jnp.where(n == 0, NEG_BIG, 0.0).astype(F32)

    qnorm = {}

    def scores(j, g):
        q = q_ref[j * WINDOW:(j + 1) * WINDOW, :]
        if j not in qnorm:
            ssq_t = lax.dot_general(indt_ref[...], q * q, (((1,), (1,)), ((), ())), preferred_element_type=F32)
            qnorm[j] = lax.rsqrt(ssq_t * (1.0 / HEAD_DIM) + EPS)
        p0 = g * pairs_per_group
        r0 = j * WINDOW
        qs = jnp.concatenate(
            [q[:, (p0 + pp) * LANES:(p0 + pp + 1) * LANES] for pp in range(pairs_per_group)], axis=0)
        kb = jnp.concatenate([k_top[g][r0:r0 + 2 * WINDOW], k_bot[g][r0:r0 + 2 * WINDOW]], axis=0)
        s_t = lax.dot_general(kb, qs, (((1,), (1,)), ((), ())), preferred_element_type=F32)
        return s_t, qnorm[j]

    def softmax_t(j, g, s_t, r_t):
        p0 = g * pairs_per_group
        prob_cols, sink_cols = [], []
        for pp in range(pairs_per_group):
            cs = slice(pp * WINDOW, (pp + 1) * WINDOW)
            for e in range(2):
                sp = s_t[e * 2 * WINDOW:e * 2 * WINDOW + WINDOW, cs]
                sc = s_t[e * 2 * WINDOW + WINDOW:(e + 1) * 2 * WINDOW, cs]
                if j == 0:
                    sp = sp + prev_pen
                comb = jnp.where(upper, sp, sc)
                hd = 2 * (p0 + pp) + e
                sink = sink_ref[hd]
                r_row = r_t[hd:hd + 1, :]
                m = jnp.maximum(jnp.max(comb, axis=0, keepdims=True) * r_row, sink)
                p = jnp.exp2(comb * r_row - m).astype(BF16)
                p_prev = p * upper_bf
                prob_cols.append(jnp.concatenate([p_prev, p - p_prev], axis=0))
                sink_cols.append(jnp.exp2(sink - m))
        return jnp.concatenate(prob_cols, axis=1), jnp.concatenate(sink_cols, axis=1)

    def outputs(j, g, p_t, sink_t):
        p0 = g * pairs_per_group
        r0 = j * WINDOW
        o_t = jnp.dot(vt_ext[g][:, r0:r0 + 2 * WINDOW], p_t, preferred_element_type=F32)
        inv = 1.0 / (o_t[HEAD_DIM:HEAD_DIM + 1, :] + sink_t)
        on = o_t[:HEAD_DIM, :] * inv
        for pp in range(pairs_per_group):
            pair = jnp.concatenate([on[:, (2 * pp) * WINDOW:(2 * pp + 1) * WINDOW],
                                    on[:, (2 * pp + 1) * WINDOW:(2 * pp + 2) * WINDOW]], axis=0)
            cols = slice((p0 + pp) * LANES, (p0 + pp + 1) * LANES)
            o_ref[j * WINDOW:(j + 1) * WINDOW, cols] = pair.T.astype(o_ref.dtype)

    units = [(j, g) for j in range(nsub) for g in range(n_groups)]
    pending = [scores(*u) for u in units[:SCORE_LOOKAHEAD]]
    probs = None
    for idx, unit in enumerate(units):
        if idx + SCORE_LOOKAHEAD < len(units):
            pending.append(scores(*units[idx + SCORE_LOOKAHEAD]))
        nxt = softmax_t(*unit, *pending.pop(0))
        if probs is not None:
            outputs(*units[idx - 1], *probs)
        probs = nxt
    outputs(*units[-1], *probs)


def _attention(qkv, sinks, k_gain_row, batch, seq, d_q):
    t = qkv.shape[0]
    steps = seq // TS_ATTN
    sub = TS_ATTN // WINDOW
    kv_blk = d_q // (2 * LANES)
    ii = np.arange(LANES)
    dd = np.arange(d_q)
    hh = np.arange(d_q // HEAD_DIM)
    bd = jnp.asarray((ii[:, None] // HEAD_DIM) == (ii[None, :] // HEAD_DIM), BF16)
    ind_t = jnp.asarray(hh[:, None] == (dd[None, :] // HEAD_DIM), BF16)
    return pl.pallas_call(
        _attn_kernel,
        grid=(batch, steps),
        in_specs=[
            pl.BlockSpec(memory_space=pltpu.SMEM),
            pl.BlockSpec((TS_ATTN, d_q), lambda b, n: (b * steps + n, 0)),
            pl.BlockSpec((TS_ATTN, 2 * LANES), lambda b, n: (b * steps + n, kv_blk)),
            pl.BlockSpec((WINDOW, 2 * LANES),
                         lambda b, n: ((b * steps + n) * sub - jnp.minimum(n, 1), kv_blk)),
            _resident((1, LANES)),
            _resident((LANES, LANES)),
            _resident(ind_t.shape),
        ],
        out_specs=pl.BlockSpec((TS_ATTN, d_q), lambda b, n: (b * steps + n, 0)),
        out_shape=jax.ShapeDtypeStruct((t, d_q), BF16),
        compiler_params=pltpu.CompilerParams(
            dimension_semantics=("parallel", "parallel"), vmem_limit_bytes=VMEM_LIMIT_BYTES),
        name="swa_attention",
    )(sinks, qkv, qkv, qkv, k_gain_row, bd, ind_t)


def _out_kernel(x_ref, attn_ref, ret_ref, za_ref, zr_ref, wba_ref, wbr_ref, wo_ref,
                wg_ref, wu_ref, wd_ref, o_ref):
    ba = jnp.dot(attn_ref[...], wba_ref[...], preferred_element_type=F32)
    br = jnp.dot(ret_ref[...], wbr_ref[...], preferred_element_type=F32)
    merged = (jax.nn.sigmoid(za_ref[...].astype(F32)) * ba
              + jax.nn.sigmoid(zr_ref[...].astype(F32)) * br)
    x1 = x_ref[...] + jnp.dot(merged.astype(BF16), wo_ref[...], preferred_element_type=F32)
    h = (x1 * lax.rsqrt(jnp.mean(x1 * x1, axis=-1, keepdims=True) + EPS)).astype(BF16)
    acc = x1
    d_ff = wg_ref.shape[1]
    for c in range(0, d_ff, FF_CHUNK):
        gate = jnp.dot(h, wg_ref[:, c:c + FF_CHUNK], preferred_element_type=F32)
        up = jnp.dot(h, wu_ref[:, c:c + FF_CHUNK], preferred_element_type=F32)
        act = (gate * jax.nn.sigmoid(gate) * up).astype(BF16)
        acc = acc + jnp.dot(act, wd_ref[c:c + FF_CHUNK, :], preferred_element_type=F32)
    o_ref[...] = acc


def _merge_ffn(x2d, attn, ret, z_a, z_r, wba, wbr, wo, wg, wu, wd):
    t, d = x2d.shape
    tok = lambda w: pl.BlockSpec((TM_OUT, w), lambda i: (i, 0))
    return pl.pallas_call(
        _out_kernel,
        grid=(t // TM_OUT,),
        in_specs=[tok(d), tok(attn.shape[1]), tok(ret.shape[1]), tok(d), tok(d),
                  _resident(wba.shape), _resident(wbr.shape), _resident(wo.shape),
                  _resident(wg.shape), _resident(wu.shape),
                  _resident(wd.shape)],
        out_specs=tok(d),
        out_shape=jax.ShapeDtypeStruct((t, d), F32),
        compiler_params=pltpu.CompilerParams(
            dimension_semantics=("parallel",), vmem_limit_bytes=VMEM_LIMIT_BYTES),
        name="merge_ffn",
    )(x2d, attn, ret, z_a, z_r, wba, wbr, wo, wg, wu, wd)


def _deinterleave_heads(w, n_heads, head_dim):
    d = w.shape[0]
    src = np.concatenate([np.arange(0, head_dim, 2), np.arange(1, head_dim, 2)])
    perm = jnp.asarray(np.arange(head_dim)[:, None] == src[None, :], w.dtype)
    out = jnp.einsum('dhk,kj->dhj', w.reshape(d, n_heads, head_dim), perm, preferred_element_type=w.dtype)
    return out.reshape(d, n_heads * head_dim)


def kernel(x, norm_mix_gain, w_in, q_norm_gain, k_norm_gain, attn_sinks, w_branch_attn, w_branch_ret,
           w_out, norm_ffn_gain, w_ffn_gate, w_ffn_up, w_ffn_down):
    batch, seq, d = x.shape
    depth = w_in.shape[0]
    att_q = w_branch_attn.shape[1]
    ret_v = w_branch_ret.shape[1]
    n_q_heads = att_q // HEAD_DIM
    att_kv = (n_q_heads // GROUP) * HEAD_DIM
    ret_heads = ret_v // RET_V_DIM
    ret_qk = ret_heads * RET_QK_DIM
    splits = np.cumsum([att_q, att_kv, att_kv, ret_qk, ret_qk, ret_v, ret_v, d])
    assert att_kv == LANES and w_in.shape[2] == splits[-1] + d

    x2d = x.reshape(batch * seq, d)
    for l in range(depth):
        w_in_b = (norm_mix_gain[l][:, None] * w_in[l]).astype(BF16)
        w_attn = w_in_b[:, :splits[2]]
        _, wq_r, wk_r, wv_r, wg_r, wz_a, wz_r = jnp.split(w_in_b, splits[2:], axis=1)
        qkv_a, z_a, z_r, ret = _proj_retention(
            x2d, _deinterleave_heads(wq_r, ret_heads, RET_QK_DIM), _deinterleave_heads(wk_r, ret_heads, RET_QK_DIM),
            wv_r, wg_r, w_attn, wz_a, wz_r, batch, seq)

        k_gain_row = jnp.tile(k_norm_gain[l] * q_norm_gain[l] * (HEAD_DIM ** -0.5 * LOG2E),
                              LANES // HEAD_DIM)[None, :]
        attn = _attention(qkv_a, attn_sinks[l] * LOG2E, k_gain_row, batch, seq, att_q)

        x2d = _merge_ffn(
            x2d, attn, ret, z_a, z_r,
            w_branch_attn[l].astype(BF16), w_branch_ret[l].astype(BF16), w_out[l].astype(BF16),
            (norm_ffn_gain[l][:, None] * w_ffn_gate[l]).astype(BF16),
            (norm_ffn_gain[l][:, None] * w_ffn_up[l]).astype(BF16), w_ffn_down[l].astype(BF16))
    return x2d.reshape(batch, seq, d)
```

```python
import functools
import math

import jax
import jax.numpy as jnp
import numpy as np
from jax import lax
from jax.experimental import pallas as pl
from jax.experimental.pallas import tpu as pltpu

F32 = jnp.float32
BF16 = jnp.bfloat16

EPS = 1e-6
HEAD_DIM = 64
GROUP = 8
WINDOW = 128
RET_QK_DIM = 256
RET_V_DIM = 512
RET_CHUNK = 256
RET_ROT_BASE = 10000.0

VMEM_LIMIT_BYTES = 56 * 1024 * 1024
LANES = 128

TM_PROJ = 512
N_CHUNK_PROJ = 512
N_CHUNK_FILL = 256
TS_ATTN = 2048
TM_OUT = 512
FF_CHUNK = 256
EARLY_CHUNKS = 1

NEG_BIG = -1e30
SUM_ROWS = 16
SCORE_LOOKAHEAD = 2
LOG2E = math.log2(math.e)


def _resident(shape):
    nd = len(shape)
    return pl.BlockSpec(shape, lambda *_: (0,) * nd, pipeline_mode=pl.Buffered(1))


def _project_tasks(h, w_ref, o_ref, chunk, silu=False, row_scale=None):
    def task(c, cw):
        y = jnp.dot(h, w_ref[:, c:c + cw], preferred_element_type=F32)
        if row_scale is not None:
            y = y * row_scale
        if silu:
            y = y * jax.nn.sigmoid(y)
        o_ref[:, c:c + cw] = y.astype(BF16)

    width = w_ref.shape[1]
    return [functools.partial(task, c, min(chunk, width - c)) for c in range(0, width, chunk)]


def _proj_ret_kernel(x_ref, wq_ref, wk_ref, wv_ref, wg_ref, wa_ref, wza_ref, wzr_ref,
                     cos_ref, sin_ref, xi_ref, ginv_ref, gchunk_ref,
                     qkv_ref, za_ref, zr_ref, ret_ref,
                     state_ref, q_s, k_s, v_s, g_s):
    n_heads = state_ref.shape[0]
    half = RET_QK_DIM // 2

    @pl.when(pl.program_id(1) == 0)
    def _():
        state_ref[...] = jnp.zeros_like(state_ref)

    x = x_ref[...]
    r = lax.rsqrt(jnp.mean(x * x, axis=-1, keepdims=True) + EPS)
    xb = x.astype(BF16)
    h = (x * r).astype(BF16)
    q_tasks = _project_tasks(h, wq_ref, q_s, N_CHUNK_PROJ)
    q_tasks[:EARLY_CHUNKS] = _project_tasks(xb, wq_ref, q_s, N_CHUNK_PROJ, row_scale=r)[:EARLY_CHUNKS]
    for task in (q_tasks + _project_tasks(h, wk_ref, k_s, N_CHUNK_PROJ)
                 + _project_tasks(h, wv_ref, v_s, N_CHUNK_PROJ)
                 + _project_tasks(h, wg_ref, g_s, N_CHUNK_PROJ, silu=True)):
        task()
    fillers = (_project_tasks(h, wa_ref, qkv_ref, N_CHUNK_FILL) + _project_tasks(h, wza_ref, za_ref, N_CHUNK_FILL)
               + _project_tasks(h, wzr_ref, zr_ref, N_CHUNK_FILL))

    ii = lax.broadcasted_iota(jnp.int32, (RET_CHUNK, RET_CHUNK), 0)
    jj = lax.broadcasted_iota(jnp.int32, (RET_CHUNK, RET_CHUNK), 1)
    causal = ii >= jj

    def rotate(t, c, s):
        t1, t2 = t[:, :half], t[:, half:]
        return jnp.concatenate([t1 * c - t2 * s, t2 * c + t1 * s], axis=1)

    def rows_of(ci):
        return slice(ci * RET_CHUNK, (ci + 1) * RET_CHUNK)

    def intra(ci, hd):
        rows = rows_of(ci)
        qk_cols = slice(hd * RET_QK_DIM, (hd + 1) * RET_QK_DIM)
        c = cos_ref[rows, :]
        s = sin_ref[rows, :]
        q = q_s[rows, qk_cols].astype(F32)
        k = k_s[rows, qk_cols].astype(F32)
        qx = (rotate(q, c, s) * xi_ref[hd]).astype(BF16)
        kinv = (rotate(k, c, s) * ginv_ref[hd]).astype(BF16)
        inner = lax.dot_general(qx, kinv, (((1,), (1,)), ((), ())), preferred_element_type=F32)
        return qx, kinv, jnp.where(causal, inner, 0.0).astype(BF16)

    def recur(ci, hd, qx, kinv, inner):
        v = v_s[rows_of(ci), hd * RET_V_DIM:(hd + 1) * RET_V_DIM]
        state = state_ref[hd]
        lhs = jnp.concatenate([inner, qx], axis=1)
        rhs = jnp.concatenate([v, state.astype(BF16)], axis=0)
        out = jnp.dot(lhs, rhs, preferred_element_type=F32)
        upd = lax.dot_general(kinv, v, (((0,), (0,)), ((), ())), preferred_element_type=F32)
        state_ref[hd] = gchunk_ref[hd] * (state + upd)
        return out

    def emit(ci, hd, out):
        rows = rows_of(ci)
        v_cols = slice(hd * RET_V_DIM, (hd + 1) * RET_V_DIM)
        normed = out * lax.rsqrt(jnp.mean(out * out, axis=-1, keepdims=True) + EPS)
        ret_ref[rows, v_cols] = (g_s[rows, v_cols].astype(F32) * normed).astype(ret_ref.dtype)

    units = [(ci, hd) for ci in range(TM_PROJ // RET_CHUNK) for hd in range(n_heads)]
    staged = intra(*units[0])
    out_prev = None
    for idx, unit in enumerate(units):
        current = staged
        if idx + 1 < len(units):
            staged = intra(*units[idx + 1])
        out = recur(*unit, *current)
        if out_prev is not None:
            emit(*units[idx - 1], out_prev)
        out_prev = out
        if fillers:
            fillers.pop(0)()
    emit(*units[-1], out_prev)
    for task in fillers:
        task()


def _proj_retention(x2d, w_qr, w_kr, w_vr, w_gr, w_attn, w_za, w_zr, batch, seq):
    t, d = x2d.shape
    n_heads = w_qr.shape[1] // RET_QK_DIM
    steps = seq // TM_PROJ
    half = RET_QK_DIM // 2
    theta = 1.0 / (RET_ROT_BASE ** jnp.linspace(0.0, 1.0, half, dtype=F32))
    ang_hi = (jnp.arange(seq // LANES, dtype=F32) * LANES)[:, None] * theta[None, :]
    ang_lo = jnp.arange(LANES, dtype=F32)[:, None] * theta[None, :]
    c_hi, s_hi = jnp.cos(ang_hi)[:, None, :], jnp.sin(ang_hi)[:, None, :]
    c_lo, s_lo = jnp.cos(ang_lo)[None, :, :], jnp.sin(ang_lo)[None, :, :]
    cos = (c_hi * c_lo - s_hi * s_lo).reshape(seq, half)
    sin = (s_hi * c_lo + c_hi * s_lo).reshape(seq, half)
    log_gamma = jnp.log(1.0 - 2.0 ** (-5.0 - jnp.arange(n_heads, dtype=F32)))
    i = jnp.arange(RET_CHUNK, dtype=F32)
    xi = jnp.exp((i + 1.0)[None, :] * log_gamma[:, None])[:, :, None]
    ginv = (jnp.exp(-(i + 1.0)[None, :] * log_gamma[:, None]) * (RET_QK_DIM ** -0.5))[:, :, None]
    gchunk = jnp.exp(RET_CHUNK * log_gamma)

    weights = [w_qr, w_kr, w_vr, w_gr, w_attn, w_za, w_zr]
    tok = lambda w: pl.BlockSpec((TM_PROJ, w), lambda b, n: (b * steps + n, 0))
    tab = pl.BlockSpec((TM_PROJ, half), lambda b, n: (n, 0))
    out_widths = [w_attn.shape[1], w_za.shape[1], w_zr.shape[1], w_vr.shape[1]]
    return pl.pallas_call(
        _proj_ret_kernel,
        grid=(batch, steps),
        in_specs=[tok(d)] + [_resident(w.shape) for w in weights]
        + [tab, tab, _resident(xi.shape), _resident(ginv.shape), pl.BlockSpec(memory_space=pltpu.SMEM)],
        out_specs=[tok(w) for w in out_widths],
        out_shape=[jax.ShapeDtypeStruct((t, w), BF16) for w in out_widths],
        scratch_shapes=[pltpu.VMEM((n_heads, RET_QK_DIM, RET_V_DIM), F32)]
        + [pltpu.VMEM((TM_PROJ, w.shape[1]), BF16) for w in (w_qr, w_kr, w_vr, w_gr)],
        compiler_params=pltpu.CompilerParams(
            dimension_semantics=("parallel", "arbitrary"), vmem_limit_bytes=VMEM_LIMIT_BYTES),
        name="proj_retention",
    )(x2d, *weights, cos, sin, xi, ginv, gchunk)


def _attn_kernel(sink_ref, q_ref, kv_ref, kvp_ref, kg_ref, bd_ref, indt_ref, o_ref):
    n = pl.program_id(1)
    nsub = TS_ATTN // WINDOW
    n_pairs = q_ref.shape[1] // LANES
    pairs_per_group = GROUP // 2
    n_groups = n_pairs // pairs_per_group

    kv_all = jnp.concatenate([kvp_ref[...], kv_ref[...]], axis=0).astype(F32)
    k_all = kv_all[:, :LANES]
    v_all = kv_all[:, LANES:]
    k_ms = jnp.dot((k_all * k_all).astype(BF16), bd_ref[...], preferred_element_type=F32) * (1.0 / HEAD_DIM)
    k_all = k_all * lax.rsqrt(k_ms + EPS) * kg_ref[...]
    lane = lax.broadcasted_iota(jnp.int32, k_all.shape, 1)
    lo = lane < HEAD_DIM
    k_sw = pltpu.roll(k_all, HEAD_DIM, axis=1)
    zero = jnp.zeros_like(k_all)
    k_top = [jnp.where(lo, k_all, zero).astype(BF16), jnp.where(lo, k_sw, zero).astype(BF16)]
    k_bot = [jnp.where(lo, zero, k_sw).astype(BF16), jnp.where(lo, zero, k_all).astype(BF16)]
    v_t = v_all.T
    ones_rows = jnp.ones((SUM_ROWS, v_t.shape[1]), F32)
    vt_ext = [jnp.concatenate([v_t[g * HEAD_DIM:(g + 1) * HEAD_DIM], ones_rows], axis=0).astype(BF16)
              for g in range(n_groups)]

    key = lax.broadcasted_iota(jnp.int32, (WINDOW, WINDOW), 0)
    qry = lax.broadcasted_iota(jnp.int32, (WINDOW, WINDOW), 1)
    upper = key > qry
    upper_bf = upper.astype(F32).astype(BF16)
    prev_pen = jnp.where(n == 0, NEG_BIG, 0.0).astype(F32)

    qnorm = {}

    def scores(j, g):
        q = q_ref[j * WINDOW:(j + 1) * WINDOW, :]
        if j not in qnorm:
            ssq_t = lax.dot_general(indt_ref[...], q * q, (((1,), (1,)), ((), ())), preferred_element_type=F32)
            qnorm[j] = lax.rsqrt(ssq_t * (1.0 / HEAD_DIM) + EPS)
        p0 = g * pairs_per_group
        r0 = j * WINDOW
        qs = jnp.concatenate(
            [q[:, (p0 + pp) * LANES:(p0 + pp + 1) * LANES] for pp in range(pairs_per_group)], axis=0)
        kb = jnp.concatenate([k_top[g][r0:r0 + 2 * WINDOW], k_bot[g][r0:r0 + 2 * WINDOW]], axis=0)
        s_t = lax.dot_general(kb, qs, (((1,), (1,)), ((), ())), preferred_element_type=F32)
        return s_t, qnorm[j]

    def softmax_t(j, g, s_t, r_t):
        p0 = g * pairs_per_group
        prob_cols, sink_cols = [], []
        for pp in range(pairs_per_group):
            cs = slice(pp * WINDOW, (pp + 1) * WINDOW)
            for e in range(2):
                sp = s_t[e * 2 * WINDOW:e * 2 * WINDOW + WINDOW, cs]
                sc = s_t[e * 2 * WINDOW + WINDOW:(e + 1) * 2 * WINDOW, cs]
                if j == 0:
                    sp = sp + prev_pen
                comb = jnp.where(upper, sp, sc)
                hd = 2 * (p0 + pp) + e
                sink = sink_ref[hd]
                r_row = r_t[hd:hd + 1, :]
                m = jnp.maximum(jnp.max(comb, axis=0, keepdims=True) * r_row, sink)
                p = jnp.exp2(comb * r_row - m).astype(BF16)
                p_prev = p * upper_bf
                prob_cols.append(jnp.concatenate([p_prev, p - p_prev], axis=0))
                sink_cols.append(jnp.exp2(sink - m))
        return jnp.concatenate(prob_cols, axis=1), jnp.concatenate(sink_cols, axis=1)

    def outputs(j, g, p_t, sink_t):
        p0 = g * pairs_per_group
        r0 = j * WINDOW
        o_t = jnp.dot(vt_ext[g][:, r0:r0 + 2 * WINDOW], p_t, preferred_element_type=F32)
        inv = 1.0 / (o_t[HEAD_DIM:HEAD_DIM + 1, :] + sink_t)
        on = o_t[:HEAD_DIM, :] * inv
        for pp in range(pairs_per_group):
            pair = jnp.concatenate([on[:, (2 * pp) * WINDOW:(2 * pp + 1) * WINDOW],
                                    on[:, (2 * pp + 1) * WINDOW:(2 * pp + 2) * WINDOW]], axis=0)
            cols = slice((p0 + pp) * LANES, (p0 + pp + 1) * LANES)
            o_ref[j * WINDOW:(j + 1) * WINDOW, cols] = pair.T.astype(o_ref.dtype)

    units = [(j, g) for j in range(nsub) for g in range(n_groups)]
    pending = [scores(*u) for u in units[:SCORE_LOOKAHEAD]]
    probs = None
    for idx, unit in enumerate(units):
        if idx + SCORE_LOOKAHEAD < len(units):
            pending.append(scores(*units[idx + SCORE_LOOKAHEAD]))
        nxt = softmax_t(*unit, *pending.pop(0))
        if probs is not None:
            outputs(*units[idx - 1], *probs)
        probs = nxt
    outputs(*units[-1], *probs)


def _attention(qkv, sinks, k_gain_row, batch, seq, d_q):
    t = qkv.shape[0]
    steps = seq // TS_ATTN
    sub = TS_ATTN // WINDOW
    kv_blk = d_q // (2 * LANES)
    ii = np.arange(LANES)
    dd = np.arange(d_q)
    hh = np.arange(d_q // HEAD_DIM)
    bd = jnp.asarray((ii[:, None] // HEAD_DIM) == (ii[None, :] // HEAD_DIM), BF16)
    ind_t = jnp.asarray(hh[:, None] == (dd[None, :] // HEAD_DIM), BF16)
    return pl.pallas_call(
        _attn_kernel,
        grid=(batch, steps),
        in_specs=[
            pl.BlockSpec(memory_space=pltpu.SMEM),
            pl.BlockSpec((TS_ATTN, d_q), lambda b, n: (b * steps + n, 0)),
            pl.BlockSpec((TS_ATTN, 2 * LANES), lambda b, n: (b * steps + n, kv_blk)),
            pl.BlockSpec((WINDOW, 2 * LANES),
                         lambda b, n: ((b * steps + n) * sub - jnp.minimum(n, 1), kv_blk)),
            _resident((1, LANES)),
            _resident((LANES, LANES)),
            _resident(ind_t.shape),
        ],
        out_specs=pl.BlockSpec((TS_ATTN, d_q), lambda b, n: (b * steps + n, 0)),
        out_shape=jax.ShapeDtypeStruct((t, d_q), BF16),
        compiler_params=pltpu.CompilerParams(
            dimension_semantics=("parallel", "parallel"), vmem_limit_bytes=VMEM_LIMIT_BYTES),
        name="swa_attention",
    )(sinks, qkv, qkv, qkv, k_gain_row, bd, ind_t)


def _out_kernel(x_ref, attn_ref, ret_ref, za_ref, zr_ref, wba_ref, wbr_ref, wo_ref,
                wg_ref, wu_ref, wd_ref, o_ref):
    ba = jnp.dot(attn_ref[...], wba_ref[...], preferred_element_type=F32)
    br = jnp.dot(ret_ref[...], wbr_ref[...], preferred_element_type=F32)
    merged = (jax.nn.sigmoid(za_ref[...].astype(F32)) * ba
              + jax.nn.sigmoid(zr_ref[...].astype(F32)) * br)
    x1 = x_ref[...] + jnp.dot(merged.astype(BF16), wo_ref[...], preferred_element_type=F32)
    r = lax.rsqrt(jnp.mean(x1 * x1, axis=-1, keepdims=True) + EPS)
    x1b = x1.astype(BF16)
    h = (x1 * r).astype(BF16)
    acc = x1
    d_ff = wg_ref.shape[1]
    for c in range(0, d_ff, FF_CHUNK):
        if c < EARLY_CHUNKS * FF_CHUNK:
            gate = jnp.dot(x1b, wg_ref[:, c:c + FF_CHUNK], preferred_element_type=F32) * r
            up = jnp.dot(x1b, wu_ref[:, c:c + FF_CHUNK], preferred_element_type=F32) * r
        else:
            gate = jnp.dot(h, wg_ref[:, c:c + FF_CHUNK], preferred_element_type=F32)
            up = jnp.dot(h, wu_ref[:, c:c + FF_CHUNK], preferred_element_type=F32)
        act = (gate * jax.nn.sigmoid(gate) * up).astype(BF16)
        acc = acc + jnp.dot(act, wd_ref[c:c + FF_CHUNK, :], preferred_element_type=F32)
    o_ref[...] = acc


def _merge_ffn(x2d, attn, ret, z_a, z_r, wba, wbr, wo, wg, wu, wd):
    t, d = x2d.shape
    tok = lambda w: pl.BlockSpec((TM_OUT, w), lambda i: (i, 0))
    return pl.pallas_call(
        _out_kernel,
        grid=(t // TM_OUT,),
        in_specs=[tok(d), tok(attn.shape[1]), tok(ret.shape[1]), tok(d), tok(d),
                  _resident(wba.shape), _resident(wbr.shape), _resident(wo.shape),
                  _resident(wg.shape), _resident(wu.shape),
                  _resident(wd.shape)],
        out_specs=tok(d),
        out_shape=jax.ShapeDtypeStruct((t, d), F32),
        compiler_params=pltpu.CompilerParams(
            dimension_semantics=("parallel",), vmem_limit_bytes=VMEM_LIMIT_BYTES),
        name="merge_ffn",
    )(x2d, attn, ret, z_a, z_r, wba, wbr, wo, wg, wu, wd)


def _deinterleave_heads(w, n_heads, head_dim):
    d = w.shape[0]
    src = np.concatenate([np.arange(0, head_dim, 2), np.arange(1, head_dim, 2)])
    perm = jnp.asarray(np.arange(head_dim)[:, None] == src[None, :], w.dtype)
    out = jnp.einsum('dhk,kj->dhj', w.reshape(d, n_heads, head_dim), perm, preferred_element_type=w.dtype)
    return out.reshape(d, n_heads * head_dim)


def kernel(x, norm_mix_gain, w_in, q_norm_gain, k_norm_gain, attn_sinks, w_branch_attn, w_branch_ret,
           w_out, norm_ffn_gain, w_ffn_gate, w_ffn_up, w_ffn_down):
    batch, seq, d = x.shape
    depth = w_in.shape[0]
    att_q = w_branch_attn.shape[1]
    ret_v = w_branch_ret.shape[1]
    n_q_heads = att_q // HEAD_DIM
    att_kv = (n_q_heads // GROUP) * HEAD_DIM
    ret_heads = ret_v // RET_V_DIM
    ret_qk = ret_heads * RET_QK_DIM
    splits = np.cumsum([att_q, att_kv, att_kv, ret_qk, ret_qk, ret_v, ret_v, d])
    assert att_kv == LANES and w_in.shape[2] == splits[-1] + d

    x2d = x.reshape(batch * seq, d)
    for l in range(depth):
        w_in_b = (norm_mix_gain[l][:, None] * w_in[l]).astype(BF16)
        w_attn = w_in_b[:, :splits[2]]
        _, wq_r, wk_r, wv_r, wg_r, wz_a, wz_r = jnp.split(w_in_b, splits[2:], axis=1)
        qkv_a, z_a, z_r, ret = _proj_retention(
            x2d, _deinterleave_heads(wq_r, ret_heads, RET_QK_DIM), _deinterleave_heads(wk_r, ret_heads, RET_QK_DIM),
            wv_r, wg_r, w_attn, wz_a, wz_r, batch, seq)

        k_gain_row = jnp.tile(k_norm_gain[l] * q_norm_gain[l] * (HEAD_DIM ** -0.5 * LOG2E),
                              LANES // HEAD_DIM)[None, :]
        attn = _attention(qkv_a, attn_sinks[l] * LOG2E, k_gain_row, batch, seq, att_q)

        x2d = _merge_ffn(
            x2d, attn, ret, z_a, z_r,
            w_branch_attn[l].astype(BF16), w_branch_ret[l].astype(BF16), w_out[l].astype(BF16),
            (norm_ffn_gain[l][:, None] * w_ffn_gate[l]).astype(BF16),
            (norm_ffn_gain[l][:, None] * w_ffn_up[l]).astype(BF16), w_ffn_down[l].astype(BF16))
    return x2d.reshape(batch, seq, d)
```

```python
import functools
import math

import jax
import jax.numpy as jnp
import numpy as np
from jax import lax
from jax.experimental import pallas as pl
from jax.experimental.pallas import tpu as pltpu

F32 = jnp.float32
BF16 = jnp.bfloat16

EPS = 1e-6
HEAD_DIM = 64
GROUP = 8
WINDOW = 128
RET_QK_DIM = 256
RET_V_DIM = 512
RET_CHUNK = 256
RET_ROT_BASE = 10000.0

VMEM_LIMIT_BYTES = 56 * 1024 * 1024
LANES = 128

TM_PROJ = 512
N_CHUNK_PROJ = 512
N_CHUNK_FILL = 256
TS_ATTN = 2048
TM_OUT = 512
FF_CHUNK = 256
EARLY_CHUNKS = 1

NEG_BIG = -1e30
SUM_ROWS = 16
SCORE_LOOKAHEAD = 2
LOG2E = math.log2(math.e)


def _resident(shape):
    nd = len(shape)
    return pl.BlockSpec(shape, lambda *_: (0,) * nd, pipeline_mode=pl.Buffered(1))


def _project_tasks(h, w_ref, o_ref, chunk, silu=False):
    def task(c, cw):
        y = jnp.dot(h, w_ref[:, c:c + cw], preferred_element_type=F32)
        if silu:
            y = y * jax.nn.sigmoid(y)
        o_ref[:, c:c + cw] = y.astype(BF16)

    width = w_ref.shape[1]
    return [functools.partial(task, c, min(chunk, width - c)) for c in range(0, width, chunk)]


def _proj_ret_kernel(x_ref, wq_ref, wk_ref, wv_ref, wg_ref, wa_ref, wza_ref, wzr_ref,
                     cos_ref, sin_ref, xi_ref, ginv_ref, gchunk_ref,
                     qkv_ref, za_ref, zr_ref, ret_ref,
                     state_ref, q_s, k_s, v_s, g_s):
    n_heads = state_ref.shape[0]
    half = RET_QK_DIM // 2

    @pl.when(pl.program_id(1) == 0)
    def _():
        state_ref[...] = jnp.zeros_like(state_ref)

    x = x_ref[...]
    h = (x * lax.rsqrt(jnp.mean(x * x, axis=-1, keepdims=True) + EPS)).astype(BF16)
    for task in (_project_tasks(h, wq_ref, q_s, N_CHUNK_PROJ) + _project_tasks(h, wk_ref, k_s, N_CHUNK_PROJ)
                 + _project_tasks(h, wv_ref, v_s, N_CHUNK_PROJ)
                 + _project_tasks(h, wg_ref, g_s, N_CHUNK_PROJ, silu=True)):
        task()
    fillers = (_project_tasks(h, wa_ref, qkv_ref, N_CHUNK_FILL) + _project_tasks(h, wza_ref, za_ref, N_CHUNK_FILL)
               + _project_tasks(h, wzr_ref, zr_ref, N_CHUNK_FILL))

    ii = lax.broadcasted_iota(jnp.int32, (RET_CHUNK, RET_CHUNK), 0)
    jj = lax.broadcasted_iota(jnp.int32, (RET_CHUNK, RET_CHUNK), 1)
    causal = ii >= jj

    def rotate(t, c, s):
        t1, t2 = t[:, :half], t[:, half:]
        return jnp.concatenate([t1 * c - t2 * s, t2 * c + t1 * s], axis=1)

    def rows_of(ci):
        return slice(ci * RET_CHUNK, (ci + 1) * RET_CHUNK)

    def intra(ci, hd):
        rows = rows_of(ci)
        qk_cols = slice(hd * RET_QK_DIM, (hd + 1) * RET_QK_DIM)
        c = cos_ref[rows, :]
        s = sin_ref[rows, :]
        q = q_s[rows, qk_cols].astype(F32)
        k = k_s[rows, qk_cols].astype(F32)
        qx = (rotate(q, c, s) * xi_ref[hd]).astype(BF16)
        kinv = (rotate(k, c, s) * ginv_ref[hd]).astype(BF16)
        inner = lax.dot_general(qx, kinv, (((1,), (1,)), ((), ())), preferred_element_type=F32)
        return qx, kinv, jnp.where(causal, inner, 0.0).astype(BF16)

    def recur(ci, hd, qx, kinv, inner):
        v = v_s[rows_of(ci), hd * RET_V_DIM:(hd + 1) * RET_V_DIM]
        state = state_ref[hd]
        lhs = jnp.concatenate([inner, qx], axis=1)
        rhs = jnp.concatenate([v, state.astype(BF16)], axis=0)
        out = jnp.dot(lhs, rhs, preferred_element_type=F32)
        upd = lax.dot_general(kinv, v, (((0,), (0,)), ((), ())), preferred_element_type=F32)
        state_ref[hd] = gchunk_ref[hd] * (state + upd)
        return out

    def emit(ci, hd, out):
        rows = rows_of(ci)
        v_cols = slice(hd * RET_V_DIM, (hd + 1) * RET_V_DIM)
        normed = out * lax.rsqrt(jnp.mean(out * out, axis=-1, keepdims=True) + EPS)
        ret_ref[rows, v_cols] = (g_s[rows, v_cols].astype(F32) * normed).astype(ret_ref.dtype)

    units = [(ci, hd) for ci in range(TM_PROJ // RET_CHUNK) for hd in range(n_heads)]
    staged = intra(*units[0])
    out_prev = None
    for idx, unit in enumerate(units):
        current = staged
        if idx + 1 < len(units):
            staged = intra(*units[idx + 1])
        out = recur(*unit, *current)
        if out_prev is not None:
            emit(*units[idx - 1], out_prev)
        out_prev = out
        if fillers:
            fillers.pop(0)()
    emit(*units[-1], out_prev)
    for task in fillers:
        task()


def _proj_retention(x2d, w_qr, w_kr, w_vr, w_gr, w_attn, w_za, w_zr, batch, seq):
    t, d = x2d.shape
    assert seq % TM_PROJ == 0 and TM_PROJ % RET_CHUNK == 0 and seq % LANES == 0
    n_heads = w_qr.shape[1] // RET_QK_DIM
    steps = seq // TM_PROJ
    half = RET_QK_DIM // 2
    theta = 1.0 / (RET_ROT_BASE ** jnp.linspace(0.0, 1.0, half, dtype=F32))
    ang_hi = (jnp.arange(seq // LANES, dtype=F32) * LANES)[:, None] * theta[None, :]
    ang_lo = jnp.arange(LANES, dtype=F32)[:, None] * theta[None, :]
    c_hi, s_hi = jnp.cos(ang_hi)[:, None, :], jnp.sin(ang_hi)[:, None, :]
    c_lo, s_lo = jnp.cos(ang_lo)[None, :, :], jnp.sin(ang_lo)[None, :, :]
    cos = (c_hi * c_lo - s_hi * s_lo).reshape(seq, half)
    sin = (s_hi * c_lo + c_hi * s_lo).reshape(seq, half)
    log_gamma = jnp.log(1.0 - 2.0 ** (-5.0 - jnp.arange(n_heads, dtype=F32)))
    i = jnp.arange(RET_CHUNK, dtype=F32)
    xi = jnp.exp((i + 1.0)[None, :] * log_gamma[:, None])[:, :, None]
    ginv = (jnp.exp(-(i + 1.0)[None, :] * log_gamma[:, None]) * (RET_QK_DIM ** -0.5))[:, :, None]
    gchunk = jnp.exp(RET_CHUNK * log_gamma)

    weights = [w_qr, w_kr, w_vr, w_gr, w_attn, w_za, w_zr]
    tok = lambda w: pl.BlockSpec((TM_PROJ, w), lambda b, n: (b * steps + n, 0))
    tab = pl.BlockSpec((TM_PROJ, half), lambda b, n: (n, 0))
    out_widths = [w_attn.shape[1], w_za.shape[1], w_zr.shape[1], w_vr.shape[1]]
    return pl.pallas_call(
        _proj_ret_kernel,
        grid=(batch, steps),
        in_specs=[tok(d)] + [_resident(w.shape) for w in weights]
        + [tab, tab, _resident(xi.shape), _resident(ginv.shape), pl.BlockSpec(memory_space=pltpu.SMEM)],
        out_specs=[tok(w) for w in out_widths],
        out_shape=[jax.ShapeDtypeStruct((t, w), BF16) for w in out_widths],
        scratch_shapes=[pltpu.VMEM((n_heads, RET_QK_DIM, RET_V_DIM), F32)]
        + [pltpu.VMEM((TM_PROJ, w.shape[1]), BF16) for w in (w_qr, w_kr, w_vr, w_gr)],
        compiler_params=pltpu.CompilerParams(
            dimension_semantics=("parallel", "arbitrary"), vmem_limit_bytes=VMEM_LIMIT_BYTES),
        name="proj_retention",
    )(x2d, *weights, cos, sin, xi, ginv, gchunk)


def _attn_kernel(sink_ref, q_ref, kv_ref, kvp_ref, kg_ref, bd_ref, indt_ref, o_ref):
    n = pl.program_id(1)
    nsub = TS_ATTN // WINDOW
    n_pairs = q_ref.shape[1] // LANES
    pairs_per_group = GROUP // 2
    n_groups = n_pairs // pairs_per_group

    kv_all = jnp.concatenate([kvp_ref[...], kv_ref[...]], axis=0).astype(F32)
    k_all = kv_all[:, :LANES]
    v_all = kv_all[:, LANES:]
    k_ms = jnp.dot((k_all * k_all).astype(BF16), bd_ref[...], preferred_element_type=F32) * (1.0 / HEAD_DIM)
    k_all = k_all * lax.rsqrt(k_ms + EPS) * kg_ref[...]
    lane = lax.broadcasted_iota(jnp.int32, k_all.shape, 1)
    lo = lane < HEAD_DIM
    k_sw = pltpu.roll(k_all, HEAD_DIM, axis=1)
    zero = jnp.zeros_like(k_all)
    k_top = [jnp.where(lo, k_all, zero).astype(BF16), jnp.where(lo, k_sw, zero).astype(BF16)]
    k_bot = [jnp.where(lo, zero, k_sw).astype(BF16), jnp.where(lo, zero, k_all).astype(BF16)]
    v_t = v_all.T
    ones_rows = jnp.ones((SUM_ROWS, v_t.shape[1]), F32)
    vt_ext = [jnp.concatenate([v_t[g * HEAD_DIM:(g + 1) * HEAD_DIM], ones_rows], axis=0).astype(BF16)
              for g in range(n_groups)]

    key = lax.broadcasted_iota(jnp.int32, (WINDOW, WINDOW), 0)
    qry = lax.broadcasted_iota(jnp.int32, (WINDOW, WINDOW), 1)
    upper = key > qry
    upper_bf = upper.astype(F32).astype(BF16)
    prev_pen = jnp.where(n == 0, NEG_BIG, 0.0).astype(F32)

    qnorm = {}

    def scores(j, g):
        q = q_ref[j * WINDOW:(j + 1) * WINDOW, :]
        if j not in qnorm:
            ssq_t = lax.dot_general(indt_ref[...], q * q, (((1,), (1,)), ((), ())), preferred_element_type=F32)
            qnorm[j] = lax.rsqrt(ssq_t * (1.0 / HEAD_DIM) + EPS)
        p0 = g * pairs_per_group
        r0 = j * WINDOW
        qs = jnp.concatenate(
            [q[:, (p0 + pp) * LANES:(p0 + pp + 1) * LANES] for pp in range(pairs_per_group)], axis=0)
        kb = jnp.concatenate([k_top[g][r0:r0 + 2 * WINDOW], k_bot[g][r0:r0 + 2 * WINDOW]], axis=0)
        s_t = lax.dot_general(kb, qs, (((1,), (1,)), ((), ())), preferred_element_type=F32)
        return s_t, qnorm[j]

    def softmax_t(j, g, s_t, r_t):
        p0 = g * pairs_per_group
        prob_cols, sink_cols = [], []
        for pp in range(pairs_per_group):
            cs = slice(pp * WINDOW, (pp + 1) * WINDOW)
            for e in range(2):
                sp = s_t[e * 2 * WINDOW:e * 2 * WINDOW + WINDOW, cs]
                sc = s_t[e * 2 * WINDOW + WINDOW:(e + 1) * 2 * WINDOW, cs]
                if j == 0:
                    sp = sp + prev_pen
                comb = jnp.where(upper, sp, sc)
                hd = 2 * (p0 + pp) + e
                sink = sink_ref[hd]
                r_row = r_t[hd:hd + 1, :]
                m = jnp.maximum(jnp.max(comb, axis=0, keepdims=True) * r_row, sink)
                p = jnp.exp2(comb * r_row - m).astype(BF16)
                p_prev = p * upper_bf
                prob_cols.append(jnp.concatenate([p_prev, p - p_prev], axis=0))
                sink_cols.append(jnp.exp2(sink - m))
        return jnp.concatenate(prob_cols, axis=1), jnp.concatenate(sink_cols, axis=1)

    def outputs(j, g, p_t, sink_t):
        p0 = g * pairs_per_group
        r0 = j * WINDOW
        o_t = jnp.dot(vt_ext[g][:, r0:r0 + 2 * WINDOW], p_t, preferred_element_type=F32)
        inv = 1.0 / (o_t[HEAD_DIM:HEAD_DIM + 1, :] + sink_t)
        on = o_t[:HEAD_DIM, :] * inv
        for pp in range(pairs_per_group):
            pair = jnp.concatenate([on[:, (2 * pp) * WINDOW:(2 * pp + 1) * WINDOW],
                                    on[:, (2 * pp + 1) * WINDOW:(2 * pp + 2) * WINDOW]], axis=0)
            cols = slice((p0 + pp) * LANES, (p0 + pp + 1) * LANES)
            o_ref[j * WINDOW:(j + 1) * WINDOW, cols] = pair.T.astype(o_ref.dtype)

    units = [(j, g) for j in range(nsub) for g in range(n_groups)]
    pending = [scores(*u) for u in units[:SCORE_LOOKAHEAD]]
    probs = None
    for idx, unit in enumerate(units):
        if idx + SCORE_LOOKAHEAD < len(units):
            pending.append(scores(*units[idx + SCORE_LOOKAHEAD]))
        nxt = softmax_t(*unit, *pending.pop(0))
        if probs is not None:
            outputs(*units[idx - 1], *probs)
        probs = nxt
    outputs(*units[-1], *probs)


def _attention(qkv, sinks, k_gain_row, batch, seq, d_q):
    t = qkv.shape[0]
    assert seq % TS_ATTN == 0 and TS_ATTN % WINDOW == 0 and qkv.shape[1] == d_q + 2 * LANES
    steps = seq // TS_ATTN
    sub = TS_ATTN // WINDOW
    kv_blk = d_q // (2 * LANES)
    ii = np.arange(LANES)
    dd = np.arange(d_q)
    hh = np.arange(d_q // HEAD_DIM)
    bd = jnp.asarray((ii[:, None] // HEAD_DIM) == (ii[None, :] // HEAD_DIM), BF16)
    ind_t = jnp.asarray(hh[:, None] == (dd[None, :] // HEAD_DIM), BF16)
    return pl.pallas_call(
        _attn_kernel,
        grid=(batch, steps),
        in_specs=[
            pl.BlockSpec(memory_space=pltpu.SMEM),
            pl.BlockSpec((TS_ATTN, d_q), lambda b, n: (b * steps + n, 0)),
            pl.BlockSpec((TS_ATTN, 2 * LANES), lambda b, n: (b * steps + n, kv_blk)),
            pl.BlockSpec((WINDOW, 2 * LANES),
                         lambda b, n: ((b * steps + n) * sub - jnp.minimum(n, 1), kv_blk)),
            _resident((1, LANES)),
            _resident((LANES, LANES)),
            _resident(ind_t.shape),
        ],
        out_specs=pl.BlockSpec((TS_ATTN, d_q), lambda b, n: (b * steps + n, 0)),
        out_shape=jax.ShapeDtypeStruct((t, d_q), BF16),
        compiler_params=pltpu.CompilerParams(
            dimension_semantics=("parallel", "parallel"), vmem_limit_bytes=VMEM_LIMIT_BYTES),
        name="swa_attention",
    )(sinks, qkv, qkv, qkv, k_gain_row, bd, ind_t)


def _out_kernel(x_ref, attn_ref, ret_ref, za_ref, zr_ref, wba_ref, wbr_ref, wo_ref,
                wg_ref, wu_ref, wd_ref, o_ref):
    ba = jnp.dot(attn_ref[...], wba_ref[...], preferred_element_type=F32)
    br = jnp.dot(ret_ref[...], wbr_ref[...], preferred_element_type=F32)
    merged = (jax.nn.sigmoid(za_ref[...].astype(F32)) * ba
              + jax.nn.sigmoid(zr_ref[...].astype(F32)) * br)
    x1 = x_ref[...] + jnp.dot(merged.astype(BF16), wo_ref[...], preferred_element_type=F32)
    r = lax.rsqrt(jnp.mean(x1 * x1, axis=-1, keepdims=True) + EPS)
    x1b = x1.astype(BF16)
    h = (x1 * r).astype(BF16)
    acc = x1
    d_ff = wg_ref.shape[1]
    for c in range(0, d_ff, FF_CHUNK):
        if c < EARLY_CHUNKS * FF_CHUNK:
            gate = jnp.dot(x1b, wg_ref[:, c:c + FF_CHUNK], preferred_element_type=F32) * r
            up = jnp.dot(x1b, wu_ref[:, c:c + FF_CHUNK], preferred_element_type=F32) * r
        else:
            gate = jnp.dot(h, wg_ref[:, c:c + FF_CHUNK], preferred_element_type=F32)
            up = jnp.dot(h, wu_ref[:, c:c + FF_CHUNK], preferred_element_type=F32)
        act = (gate * jax.nn.sigmoid(gate) * up).astype(BF16)
        acc = acc + jnp.dot(act, wd_ref[c:c + FF_CHUNK, :], preferred_element_type=F32)
    o_ref[...] = acc


def _merge_ffn(x2d, attn, ret, z_a, z_r, wba, wbr, wo, wg, wu, wd):
    t, d = x2d.shape
    assert t % TM_OUT == 0 and wg.shape[1] % FF_CHUNK == 0
    tok = lambda w: pl.BlockSpec((TM_OUT, w), lambda i: (i, 0))
    return pl.pallas_call(
        _out_kernel,
        grid=(t // TM_OUT,),
        in_specs=[tok(d), tok(attn.shape[1]), tok(ret.shape[1]), tok(d), tok(d),
                  _resident(wba.shape), _resident(wbr.shape), _resident(wo.shape),
                  _resident(wg.shape), _resident(wu.shape),
                  _resident(wd.shape)],
        out_specs=tok(d),
        out_shape=jax.ShapeDtypeStruct((t, d), F32),
        compiler_params=pltpu.CompilerParams(
            dimension_semantics=("parallel",), vmem_limit_bytes=VMEM_LIMIT_BYTES),
        name="merge_ffn",
    )(x2d, attn, ret, z_a, z_r, wba, wbr, wo, wg, wu, wd)


def _deinterleave_heads(w, n_heads, head_dim):
    d = w.shape[0]
    src = np.concatenate([np.arange(0, head_dim, 2), np.arange(1, head_dim, 2)])
    perm = jnp.asarray(np.arange(head_dim)[:, None] == src[None, :], w.dtype)
    out = jnp.einsum('dhk,kj->dhj', w.reshape(d, n_heads, head_dim), perm, preferred_element_type=w.dtype)
    return out.reshape(d, n_heads * head_dim)


def kernel(x, norm_mix_gain, w_in, q_norm_gain, k_norm_gain, attn_sinks, w_branch_attn, w_branch_ret,
           w_out, norm_ffn_gain, w_ffn_gate, w_ffn_up, w_ffn_down):
    batch, seq, d = x.shape
    depth = w_in.shape[0]
    att_q = w_branch_attn.shape[1]
    ret_v = w_branch_ret.shape[1]
    n_q_heads = att_q // HEAD_DIM
    att_kv = (n_q_heads // GROUP) * HEAD_DIM
    ret_heads = ret_v // RET_V_DIM
    ret_qk = ret_heads * RET_QK_DIM
    splits = np.cumsum([att_q, att_kv, att_kv, ret_qk, ret_qk, ret_v, ret_v, d])
    assert att_kv == LANES and w_in.shape[2] == splits[-1] + d

    x2d = x.reshape(batch * seq, d)
    for l in range(depth):
        w_in_b = (norm_mix_gain[l][:, None] * w_in[l]).astype(BF16)
        w_attn = w_in_b[:, :splits[2]]
        _, wq_r, wk_r, wv_r, wg_r, wz_a, wz_r = jnp.split(w_in_b, splits[2:], axis=1)
        qkv_a, z_a, z_r, ret = _proj_retention(
            x2d, _deinterleave_heads(wq_r, ret_heads, RET_QK_DIM), _deinterleave_heads(wk_r, ret_heads, RET_QK_DIM),
            wv_r, wg_r, w_attn, wz_a, wz_r, batch, seq)

        k_gain_row = jnp.tile(k_norm_gain[l] * q_norm_gain[l] * (HEAD_DIM ** -0.5 * LOG2E),
                              LANES // HEAD_DIM)[None, :]
        attn = _attention(qkv_a, attn_sinks[l] * LOG2E, k_gain_row, batch, seq, att_q)

        x2d = _merge_ffn(
            x2d, attn, ret, z_a, z_r,
            w_branch_attn[l].astype(BF16), w_branch_ret[l].astype(BF16), w_out[l].astype(BF16),
            (norm_ffn_gain[l][:, None] * w_ffn_gate[l]).astype(BF16),
            (norm_ffn_gain[l][:, None] * w_ffn_up[l]).astype(BF16), w_ffn_down[l].astype(BF16))
    return x2d.reshape(batch, seq, d)
```

```python
import functools
import math

import jax
import jax.numpy as jnp
import numpy as np
from jax import lax
from jax.experimental import pallas as pl
from jax.experimental.pallas import tpu as pltpu

F32 = jnp.float32
BF16 = jnp.bfloat16

EPS = 1e-6
HEAD_DIM = 64
GROUP = 8
WINDOW = 128
RET_QK_DIM = 256
RET_V_DIM = 512
RET_CHUNK = 256
RET_ROT_BASE = 10000.0

VMEM_LIMIT_BYTES = 56 * 1024 * 1024
LANES = 128

TM_PROJ = 512
N_CHUNK_PROJ = 512
N_CHUNK_FILL = 256
TS_ATTN = 2048
TM_OUT = 512
FF_CHUNK = 256
EARLY_CHUNKS = 1

NEG_BIG = -1e30
SUM_ROWS = 16
SCORE_LOOKAHEAD = 2
LOG2E = math.log2(math.e)


def _resident(shape):
    nd = len(shape)
    return pl.BlockSpec(shape, lambda *_: (0,) * nd, pipeline_mode=pl.Buffered(1))


def _project_tasks(h, w_ref, o_ref, chunk, silu=False):
    def task(c, cw):
        y = jnp.dot(h, w_ref[:, c:c + cw], preferred_element_type=F32)
        if silu:
            y = y * jax.nn.sigmoid(y)
        o_ref[:, c:c + cw] = y.astype(BF16)

    width = w_ref.shape[1]
    return [functools.partial(task, c, min(chunk, width - c)) for c in range(0, width, chunk)]


def _proj_ret_kernel(x_ref, wq_ref, wk_ref, wv_ref, wg_ref, wa_ref, wz_ref,
                     cos_ref, sin_ref, xi_ref, ginv_ref, gchunk_ref,
                     qkv_ref, zz_ref, ret_ref,
                     state_ref, q_s, k_s, v_s, g_s):
    n_heads = state_ref.shape[0]
    half = RET_QK_DIM // 2

    @pl.when(pl.program_id(1) == 0)
    def _():
        state_ref[...] = jnp.zeros_like(state_ref)

    x = x_ref[...]
    h = (x * lax.rsqrt(jnp.mean(x * x, axis=-1, keepdims=True) + EPS)).astype(BF16)
    for task in (_project_tasks(h, wq_ref, q_s, N_CHUNK_PROJ) + _project_tasks(h, wk_ref, k_s, N_CHUNK_PROJ)
                 + _project_tasks(h, wv_ref, v_s, N_CHUNK_PROJ)
                 + _project_tasks(h, wg_ref, g_s, N_CHUNK_PROJ, silu=True)):
        task()
    fillers = _project_tasks(h, wa_ref, qkv_ref, N_CHUNK_FILL) + _project_tasks(h, wz_ref, zz_ref, N_CHUNK_FILL)

    ii = lax.broadcasted_iota(jnp.int32, (RET_CHUNK, RET_CHUNK), 0)
    jj = lax.broadcasted_iota(jnp.int32, (RET_CHUNK, RET_CHUNK), 1)
    causal = ii >= jj

    def rotate(t, c, s):
        t1, t2 = t[:, :half], t[:, half:]
        return jnp.concatenate([t1 * c - t2 * s, t2 * c + t1 * s], axis=1)

    def rows_of(ci):
        return slice(ci * RET_CHUNK, (ci + 1) * RET_CHUNK)

    def intra(ci, hd):
        rows = rows_of(ci)
        qk_cols = slice(hd * RET_QK_DIM, (hd + 1) * RET_QK_DIM)
        c = cos_ref[rows, :]
        s = sin_ref[rows, :]
        q = q_s[rows, qk_cols].astype(F32)
        k = k_s[rows, qk_cols].astype(F32)
        qx = (rotate(q, c, s) * xi_ref[hd]).astype(BF16)
        kinv = (rotate(k, c, s) * ginv_ref[hd]).astype(BF16)
        inner = lax.dot_general(qx, kinv, (((1,), (1,)), ((), ())), preferred_element_type=F32)
        return qx, kinv, jnp.where(causal, inner, 0.0).astype(BF16)

    def recur(ci, hd, qx, kinv, inner):
        v = v_s[rows_of(ci), hd * RET_V_DIM:(hd + 1) * RET_V_DIM]
        state = state_ref[hd]
        lhs = jnp.concatenate([inner, qx], axis=1)
        rhs = jnp.concatenate([v, state.astype(BF16)], axis=0)
        out = jnp.dot(lhs, rhs, preferred_element_type=F32)
        upd = lax.dot_general(kinv, v, (((0,), (0,)), ((), ())), preferred_element_type=F32)
        state_ref[hd] = gchunk_ref[hd] * (state + upd)
        return out

    def emit(ci, hd, out):
        rows = rows_of(ci)
        v_cols = slice(hd * RET_V_DIM, (hd + 1) * RET_V_DIM)
        normed = out * lax.rsqrt(jnp.mean(out * out, axis=-1, keepdims=True) + EPS)
        ret_ref[rows, v_cols] = (g_s[rows, v_cols].astype(F32) * normed).astype(ret_ref.dtype)

    units = [(ci, hd) for ci in range(TM_PROJ // RET_CHUNK) for hd in range(n_heads)]
    staged = intra(*units[0])
    out_prev = None
    for idx, unit in enumerate(units):
        current = staged
        if idx + 1 < len(units):
            staged = intra(*units[idx + 1])
        out = recur(*unit, *current)
        if fillers:
            fillers.pop(0)()
        if out_prev is not None:
            emit(*units[idx - 1], out_prev)
        out_prev = out
        if len(fillers) > len(units) - idx:
            fillers.pop(0)()
    emit(*units[-1], out_prev)
    for task in fillers:
        task()


def _proj_retention(x2d, w_qr, w_kr, w_vr, w_gr, w_attn, w_z, batch, seq):
    t, d = x2d.shape
    assert seq % TM_PROJ == 0 and TM_PROJ % RET_CHUNK == 0 and seq % LANES == 0
    n_heads = w_qr.shape[1] // RET_QK_DIM
    steps = seq // TM_PROJ
    half = RET_QK_DIM // 2
    theta = 1.0 / (RET_ROT_BASE ** jnp.linspace(0.0, 1.0, half, dtype=F32))
    ang_hi = (jnp.arange(seq // LANES, dtype=F32) * LANES)[:, None] * theta[None, :]
    ang_lo = jnp.arange(LANES, dtype=F32)[:, None] * theta[None, :]
    c_hi, s_hi = jnp.cos(ang_hi)[:, None, :], jnp.sin(ang_hi)[:, None, :]
    c_lo, s_lo = jnp.cos(ang_lo)[None, :, :], jnp.sin(ang_lo)[None, :, :]
    cos = (c_hi * c_lo - s_hi * s_lo).reshape(seq, half)
    sin = (s_hi * c_lo + c_hi * s_lo).reshape(seq, half)
    log_gamma = jnp.log(1.0 - 2.0 ** (-5.0 - jnp.arange(n_heads, dtype=F32)))
    i = jnp.arange(RET_CHUNK, dtype=F32)
    xi = jnp.exp((i + 1.0)[None, :] * log_gamma[:, None])[:, :, None]
    ginv = (jnp.exp(-(i + 1.0)[None, :] * log_gamma[:, None]) * (RET_QK_DIM ** -0.5))[:, :, None]
    gchunk = jnp.exp(RET_CHUNK * log_gamma)

    weights = [w_qr, w_kr, w_vr, w_gr, w_attn, w_z]
    tok = lambda w: pl.BlockSpec((TM_PROJ, w), lambda b, n: (b * steps + n, 0))
    tab = pl.BlockSpec((TM_PROJ, half), lambda b, n: (n, 0))
    out_widths = [w_attn.shape[1], w_z.shape[1], w_vr.shape[1]]
    return pl.pallas_call(
        _proj_ret_kernel,
        grid=(batch, steps),
        in_specs=[tok(d)] + [_resident(w.shape) for w in weights]
        + [tab, tab, _resident(xi.shape), _resident(ginv.shape), pl.BlockSpec(memory_space=pltpu.SMEM)],
        out_specs=[tok(w) for w in out_widths],
        out_shape=[jax.ShapeDtypeStruct((t, w), BF16) for w in out_widths],
        scratch_shapes=[pltpu.VMEM((n_heads, RET_QK_DIM, RET_V_DIM), F32)]
        + [pltpu.VMEM((TM_PROJ, w.shape[1]), BF16) for w in (w_qr, w_kr, w_vr, w_gr)],
        compiler_params=pltpu.CompilerParams(
            dimension_semantics=("parallel", "arbitrary"), vmem_limit_bytes=VMEM_LIMIT_BYTES),
        name="proj_retention",
    )(x2d, *weights, cos, sin, xi, ginv, gchunk)


def _attn_kernel(sink_ref, q_ref, kv_ref, kvp_ref, kg_ref, bd_ref, indt_ref, o_ref):
    n = pl.program_id(1)
    nsub = TS_ATTN // WINDOW
    n_pairs = q_ref.shape[1] // LANES
    pairs_per_group = GROUP // 2
    n_groups = n_pairs // pairs_per_group

    kv_all = jnp.concatenate([kvp_ref[...], kv_ref[...]], axis=0).astype(F32)
    k_all = kv_all[:, :LANES]
    v_all = kv_all[:, LANES:]
    k_ms = jnp.dot((k_all * k_all).astype(BF16), bd_ref[...], preferred_element_type=F32) * (1.0 / HEAD_DIM)
    k_all = k_all * lax.rsqrt(k_ms + EPS) * kg_ref[...]
    lane = lax.broadcasted_iota(jnp.int32, k_all.shape, 1)
    lo = lane < HEAD_DIM
    k_sw = pltpu.roll(k_all, HEAD_DIM, axis=1)
    zero = jnp.zeros_like(k_all)
    k_top = [jnp.where(lo, k_all, zero).astype(BF16), jnp.where(lo, k_sw, zero).astype(BF16)]
    k_bot = [jnp.where(lo, zero, k_sw).astype(BF16), jnp.where(lo, zero, k_all).astype(BF16)]
    v_t = v_all.T
    ones_rows = jnp.ones((SUM_ROWS, v_t.shape[1]), F32)
    vt_ext = [jnp.concatenate([v_t[g * HEAD_DIM:(g + 1) * HEAD_DIM], ones_rows], axis=0).astype(BF16)
              for g in range(n_groups)]

    key = lax.broadcasted_iota(jnp.int32, (WINDOW, WINDOW), 0)
    qry = lax.broadcasted_iota(jnp.int32, (WINDOW, WINDOW), 1)
    upper = key > qry
    upper_bf = upper.astype(F32).astype(BF16)
    prev_pen = jnp.where(n == 0, NEG_BIG, 0.0).astype(F32)

    qnorm = {}

    def scores(j, g):
        q = q_ref[j * WINDOW:(j + 1) * WINDOW, :]
        if j not in qnorm:
            ssq_t = lax.dot_general(indt_ref[...], q * q, (((1,), (1,)), ((), ())), preferred_element_type=F32)
            qnorm[j] = lax.rsqrt(ssq_t * (1.0 / HEAD_DIM) + EPS)
        p0 = g * pairs_per_group
        r0 = j * WINDOW
        qs = jnp.concatenate(
            [q[:, (p0 + pp) * LANES:(p0 + pp + 1) * LANES] for pp in range(pairs_per_group)], axis=0)
        kb = jnp.concatenate([k_top[g][r0:r0 + 2 * WINDOW], k_bot[g][r0:r0 + 2 * WINDOW]], axis=0)
        s_t = lax.dot_general(kb, qs, (((1,), (1,)), ((), ())), preferred_element_type=F32)
        return s_t, qnorm[j]

    def softmax_t(j, g, s_t, r_t):
        p0 = g * pairs_per_group
        prob_cols, sink_cols = [], []
        for pp in range(pairs_per_group):
            cs = slice(pp * WINDOW, (pp + 1) * WINDOW)
            for e in range(2):
                sp = s_t[e * 2 * WINDOW:e * 2 * WINDOW + WINDOW, cs]
                sc = s_t[e * 2 * WINDOW + WINDOW:(e + 1) * 2 * WINDOW, cs]
                if j == 0:
                    sp = sp + prev_pen
                comb = jnp.where(upper, sp, sc)
                hd = 2 * (p0 + pp) + e
                sink = sink_ref[hd]
                r_row = r_t[hd:hd + 1, :]
                m = jnp.maximum(jnp.max(comb, axis=0, keepdims=True) * r_row, sink)
                p = jnp.exp2(comb * r_row - m).astype(BF16)
                p_prev = p * upper_bf
                prob_cols.append(jnp.concatenate([p_prev, p - p_prev], axis=0))
                sink_cols.append(jnp.exp2(sink - m))
        return jnp.concatenate(prob_cols, axis=1), jnp.concatenate(sink_cols, axis=1)

    def outputs(j, g, p_t, sink_t):
        p0 = g * pairs_per_group
        r0 = j * WINDOW
        o_t = jnp.dot(vt_ext[g][:, r0:r0 + 2 * WINDOW], p_t, preferred_element_type=F32)
        inv = 1.0 / (o_t[HEAD_DIM:HEAD_DIM + 1, :] + sink_t)
        on = o_t[:HEAD_DIM, :] * inv
        for pp in range(pairs_per_group):
            pair = jnp.concatenate([on[:, (2 * pp) * WINDOW:(2 * pp + 1) * WINDOW],
                                    on[:, (2 * pp + 1) * WINDOW:(2 * pp + 2) * WINDOW]], axis=0)
            cols = slice((p0 + pp) * LANES, (p0 + pp + 1) * LANES)
            o_ref[j * WINDOW:(j + 1) * WINDOW, cols] = pair.T.astype(o_ref.dtype)

    units = [(j, g) for j in range(nsub) for g in range(n_groups)]
    pending = [scores(*u) for u in units[:SCORE_LOOKAHEAD]]
    probs = None
    for idx, unit in enumerate(units):
        if idx + SCORE_LOOKAHEAD < len(units):
            pending.append(scores(*units[idx + SCORE_LOOKAHEAD]))
        nxt = softmax_t(*unit, *pending.pop(0))
        if probs is not None:
            outputs(*units[idx - 1], *probs)
        probs = nxt
    outputs(*units[-1], *probs)


def _attention(qkv, sinks, k_gain_row, batch, seq, d_q):
    t = qkv.shape[0]
    assert seq % TS_ATTN == 0 and TS_ATTN % WINDOW == 0 and qkv.shape[1] == d_q + 2 * LANES
    steps = seq // TS_ATTN
    sub = TS_ATTN // WINDOW
    kv_blk = d_q // (2 * LANES)
    ii = np.arange(LANES)
    dd = np.arange(d_q)
    hh = np.arange(d_q // HEAD_DIM)
    bd = jnp.asarray((ii[:, None] // HEAD_DIM) == (ii[None, :] // HEAD_DIM), BF16)
    ind_t = jnp.asarray(hh[:, None] == (dd[None, :] // HEAD_DIM), BF16)
    return pl.pallas_call(
        _attn_kernel,
        grid=(batch, steps),
        in_specs=[
            pl.BlockSpec(memory_space=pltpu.SMEM),
            pl.BlockSpec((TS_ATTN, d_q), lambda b, n: (b * steps + n, 0)),
            pl.BlockSpec((TS_ATTN, 2 * LANES), lambda b, n: (b * steps + n, kv_blk)),
            pl.BlockSpec((WINDOW, 2 * LANES),
                         lambda b, n: ((b * steps + n) * sub - jnp.minimum(n, 1), kv_blk)),
            _resident((1, LANES)),
            _resident((LANES, LANES)),
            _resident(ind_t.shape),
        ],
        out_specs=pl.BlockSpec((TS_ATTN, d_q), lambda b, n: (b * steps + n, 0)),
        out_shape=jax.ShapeDtypeStruct((t, d_q), BF16),
        compiler_params=pltpu.CompilerParams(
            dimension_semantics=("parallel", "parallel"), vmem_limit_bytes=VMEM_LIMIT_BYTES),
        name="swa_attention",
    )(sinks, qkv, qkv, qkv, k_gain_row, bd, ind_t)


def _out_kernel(x_ref, attn_ref, ret_ref, zz_ref, wba_ref, wbr_ref, wo_ref,
                wg_ref, wu_ref, wd_ref, o_ref):
    d = x_ref.shape[1]
    ba = jnp.dot(attn_ref[...], wba_ref[...], preferred_element_type=F32)
    br = jnp.dot(ret_ref[...], wbr_ref[...], preferred_element_type=F32)
    merged = (jax.nn.sigmoid(zz_ref[:, :d].astype(F32)) * ba
              + jax.nn.sigmoid(zz_ref[:, d:].astype(F32)) * br)
    x1 = x_ref[...] + jnp.dot(merged.astype(BF16), wo_ref[...], preferred_element_type=F32)
    r = lax.rsqrt(jnp.mean(x1 * x1, axis=-1, keepdims=True) + EPS)
    x1b = x1.astype(BF16)
    h = (x1 * r).astype(BF16)
    acc = x1
    d_ff = wg_ref.shape[1]
    for c in range(0, d_ff, FF_CHUNK):
        if c < EARLY_CHUNKS * FF_CHUNK:
            gate = jnp.dot(x1b, wg_ref[:, c:c + FF_CHUNK], preferred_element_type=F32) * r
            up = jnp.dot(x1b, wu_ref[:, c:c + FF_CHUNK], preferred_element_type=F32) * r
        else:
            gate = jnp.dot(h, wg_ref[:, c:c + FF_CHUNK], preferred_element_type=F32)
            up = jnp.dot(h, wu_ref[:, c:c + FF_CHUNK], preferred_element_type=F32)
        act = (gate * jax.nn.sigmoid(gate) * up).astype(BF16)
        acc = acc + jnp.dot(act, wd_ref[c:c + FF_CHUNK, :], preferred_element_type=F32)
    o_ref[...] = acc


def _merge_ffn(x2d, attn, ret, zz, wba, wbr, wo, wg, wu, wd):
    t, d = x2d.shape
    assert t % TM_OUT == 0 and wg.shape[1] % FF_CHUNK == 0
    tok = lambda w: pl.BlockSpec((TM_OUT, w), lambda i: (i, 0))
    return pl.pallas_call(
        _out_kernel,
        grid=(t // TM_OUT,),
        in_specs=[tok(d), tok(attn.shape[1]), tok(ret.shape[1]), tok(zz.shape[1]),
                  _resident(wba.shape), _resident(wbr.shape), _resident(wo.shape),
                  _resident(wg.shape), _resident(wu.shape),
                  _resident(wd.shape)],
        out_specs=tok(d),
        out_shape=jax.ShapeDtypeStruct((t, d), F32),
        compiler_params=pltpu.CompilerParams(
            dimension_semantics=("parallel",), vmem_limit_bytes=VMEM_LIMIT_BYTES),
        name="merge_ffn",
    )(x2d, attn, ret, zz, wba, wbr, wo, wg, wu, wd)


def _deinterleave_heads(w, n_heads, head_dim):
    d = w.shape[0]
    src = np.concatenate([np.arange(0, head_dim, 2), np.arange(1, head_dim, 2)])
    perm = jnp.asarray(np.arange(head_dim)[:, None] == src[None, :], w.dtype)
    out = jnp.einsum('dhk,kj->dhj', w.reshape(d, n_heads, head_dim), perm, preferred_element_type=w.dtype)
    return out.reshape(d, n_heads * head_dim)


def kernel(x, norm_mix_gain, w_in, q_norm_gain, k_norm_gain, attn_sinks, w_branch_attn, w_branch_ret,
           w_out, norm_ffn_gain, w_ffn_gate, w_ffn_up, w_ffn_down):
    batch, seq, d = x.shape
    depth = w_in.shape[0]
    att_q = w_branch_attn.shape[1]
    ret_v = w_branch_ret.shape[1]
    n_q_heads = att_q // HEAD_DIM
    att_kv = (n_q_heads // GROUP) * HEAD_DIM
    ret_heads = ret_v // RET_V_DIM
    ret_qk = ret_heads * RET_QK_DIM
    splits = np.cumsum([att_q, att_kv, att_kv, ret_qk, ret_qk, ret_v, ret_v, d])
    assert att_kv == LANES and w_in.shape[2] == splits[-1] + d

    x2d = x.reshape(batch * seq, d)
    for l in range(depth):
        w_in_b = (norm_mix_gain[l][:, None] * w_in[l]).astype(BF16)
        w_attn = w_in_b[:, :splits[2]]
        _, wq_r, wk_r, wv_r, wg_r, w_z = jnp.split(w_in_b, splits[2:7], axis=1)
        qkv_a, zz, ret = _proj_retention(
            x2d, _deinterleave_heads(wq_r, ret_heads, RET_QK_DIM), _deinterleave_heads(wk_r, ret_heads, RET_QK_DIM),
            wv_r, wg_r, w_attn, w_z, batch, seq)

        k_gain_row = jnp.tile(k_norm_gain[l] * q_norm_gain[l] * (HEAD_DIM ** -0.5 * LOG2E),
                              LANES // HEAD_DIM)[None, :]
        attn = _attention(qkv_a, attn_sinks[l] * LOG2E, k_gain_row, batch, seq, att_q)

        x2d = _merge_ffn(
            x2d, attn, ret, zz,
            w_branch_attn[l].astype(BF16), w_branch_ret[l].astype(BF16), w_out[l].astype(BF16),
            (norm_ffn_gain[l][:, None] * w_ffn_gate[l]).astype(BF16),
            (norm_ffn_gain[l][:, None] * w_ffn_up[l]).astype(BF16), w_ffn_down[l].astype(BF16))
    return x2d.reshape(batch, seq, d)
```

```python
import functools
import math

import jax
import jax.numpy as jnp
import numpy as np
from jax import lax
from jax.experimental import pallas as pl
from jax.experimental.pallas import tpu as pltpu

F32 = jnp.float32
BF16 = jnp.bfloat16

EPS = 1e-6
HEAD_DIM = 64
GROUP = 8
WINDOW = 128
RET_QK_DIM = 256
RET_V_DIM = 512
RET_CHUNK = 256
RET_ROT_BASE = 10000.0

VMEM_LIMIT_BYTES = 56 * 1024 * 1024
LANES = 128

TM_PROJ = 512
N_CHUNK_PROJ = 512
N_CHUNK_FILL = 256
TS_ATTN = 2048
TM_OUT = 512
FF_CHUNK = 256
EARLY_CHUNKS = 1

NEG_BIG = -1e30
SUM_ROWS = 16
SCORE_LOOKAHEAD = 2
LOG2E = math.log2(math.e)


def _resident(shape):
    nd = len(shape)
    return pl.BlockSpec(shape, lambda *_: (0,) * nd, pipeline_mode=pl.Buffered(1))


def _project_tasks(h, w_ref, o_ref, chunk, silu=False):
    def task(c, cw):
        y = jnp.dot(h, w_ref[:, c:c + cw], preferred_element_type=F32)
        if silu:
            y = y * jax.nn.sigmoid(y)
        o_ref[:, c:c + cw] = y.astype(BF16)

    width = w_ref.shape[1]
    return [functools.partial(task, c, min(chunk, width - c)) for c in range(0, width, chunk)]


def _proj_ret_kernel(x_ref, wq_ref, wk_ref, wv_ref, wg_ref, wa_ref, wz_ref,
                     cos_ref, sin_ref, xi_ref, ginv_ref, gchunk_ref,
                     qkv_ref, zz_ref, ret_ref,
                     state_ref, q_s, k_s, v_s, g_s):
    n_heads = state_ref.shape[0]
    half = RET_QK_DIM // 2

    @pl.when(pl.program_id(1) == 0)
    def _():
        state_ref[...] = jnp.zeros_like(state_ref)

    x = x_ref[...]
    h = (x * lax.rsqrt(jnp.mean(x * x, axis=-1, keepdims=True) + EPS)).astype(BF16)
    for task in (_project_tasks(h, wq_ref, q_s, N_CHUNK_PROJ) + _project_tasks(h, wk_ref, k_s, N_CHUNK_PROJ)
                 + _project_tasks(h, wv_ref, v_s, N_CHUNK_PROJ)
                 + _project_tasks(h, wg_ref, g_s, N_CHUNK_PROJ, silu=True)):
        task()
    fillers = _project_tasks(h, wa_ref, qkv_ref, N_CHUNK_FILL) + _project_tasks(h, wz_ref, zz_ref, N_CHUNK_FILL)

    ii = lax.broadcasted_iota(jnp.int32, (RET_CHUNK, RET_CHUNK), 0)
    jj = lax.broadcasted_iota(jnp.int32, (RET_CHUNK, RET_CHUNK), 1)
    causal = ii >= jj

    def rotate(t, c, s):
        t1, t2 = t[:, :half], t[:, half:]
        return jnp.concatenate([t1 * c - t2 * s, t2 * c + t1 * s], axis=1)

    def rows_of(ci):
        return slice(ci * RET_CHUNK, (ci + 1) * RET_CHUNK)

    def intra(ci, hd):
        rows = rows_of(ci)
        qk_cols = slice(hd * RET_QK_DIM, (hd + 1) * RET_QK_DIM)
        c = cos_ref[rows, :]
        s = sin_ref[rows, :]
        q = q_s[rows, qk_cols].astype(F32)
        k = k_s[rows, qk_cols].astype(F32)
        qx = (rotate(q, c, s) * xi_ref[hd]).astype(BF16)
        kinv = (rotate(k, c, s) * ginv_ref[hd]).astype(BF16)
        inner = lax.dot_general(qx, kinv, (((1,), (1,)), ((), ())), preferred_element_type=F32)
        return qx, kinv, jnp.where(causal, inner, 0.0).astype(BF16)

    def recur(ci, hd, qx, kinv, inner):
        v = v_s[rows_of(ci), hd * RET_V_DIM:(hd + 1) * RET_V_DIM]
        state = state_ref[hd]
        lhs = jnp.concatenate([inner, qx], axis=1)
        rhs = jnp.concatenate([v, state.astype(BF16)], axis=0)
        out = jnp.dot(lhs, rhs, preferred_element_type=F32)
        upd = lax.dot_general(kinv, v, (((0,), (0,)), ((), ())), preferred_element_type=F32)
        state_ref[hd] = gchunk_ref[hd] * (state + upd)
        return out

    def emit(ci, hd, out):
        rows = rows_of(ci)
        v_cols = slice(hd * RET_V_DIM, (hd + 1) * RET_V_DIM)
        normed = out * lax.rsqrt(jnp.mean(out * out, axis=-1, keepdims=True) + EPS)
        ret_ref[rows, v_cols] = (g_s[rows, v_cols].astype(F32) * normed).astype(ret_ref.dtype)

    units = [(ci, hd) for ci in range(TM_PROJ // RET_CHUNK) for hd in range(n_heads)]
    staged = intra(*units[0])
    out_prev = None
    for idx, unit in enumerate(units):
        current = staged
        if idx + 1 < len(units):
            staged = intra(*units[idx + 1])
        out = recur(*unit, *current)
        if fillers:
            fillers.pop(0)()
        if out_prev is not None:
            emit(*units[idx - 1], out_prev)
        out_prev = out
        if len(fillers) > len(units) - idx:
            fillers.pop(0)()
    emit(*units[-1], out_prev)
    for task in fillers:
        task()


def _proj_retention(x2d, w_qr, w_kr, w_vr, w_gr, w_attn, w_z, batch, seq):
    t, d = x2d.shape
    assert seq % TM_PROJ == 0 and TM_PROJ % RET_CHUNK == 0 and seq % LANES == 0
    n_heads = w_qr.shape[1] // RET_QK_DIM
    steps = seq // TM_PROJ
    half = RET_QK_DIM // 2
    theta = 1.0 / (RET_ROT_BASE ** jnp.linspace(0.0, 1.0, half, dtype=F32))
    ang_hi = (jnp.arange(seq // LANES, dtype=F32) * LANES)[:, None] * theta[None, :]
    ang_lo = jnp.arange(LANES, dtype=F32)[:, None] * theta[None, :]
    c_hi, s_hi = jnp.cos(ang_hi)[:, None, :], jnp.sin(ang_hi)[:, None, :]
    c_lo, s_lo = jnp.cos(ang_lo)[None, :, :], jnp.sin(ang_lo)[None, :, :]
    cos = (c_hi * c_lo - s_hi * s_lo).reshape(seq, half)
    sin = (s_hi * c_lo + c_hi * s_lo).reshape(seq, half)
    log_gamma = jnp.log(1.0 - 2.0 ** (-5.0 - jnp.arange(n_heads, dtype=F32)))
    i = jnp.arange(RET_CHUNK, dtype=F32)
    xi = jnp.exp((i + 1.0)[None, :] * log_gamma[:, None])[:, :, None]
    ginv = (jnp.exp(-(i + 1.0)[None, :] * log_gamma[:, None]) * (RET_QK_DIM ** -0.5))[:, :, None]
    gchunk = jnp.exp(RET_CHUNK * log_gamma)

    weights = [w_qr, w_kr, w_vr, w_gr, w_attn, w_z]
    tok = lambda w: pl.BlockSpec((TM_PROJ, w), lambda b, n: (b * steps + n, 0))
    tab = pl.BlockSpec((TM_PROJ, half), lambda b, n: (n, 0))
    out_widths = [w_attn.shape[1], w_z.shape[1], w_vr.shape[1]]
    return pl.pallas_call(
        _proj_ret_kernel,
        grid=(batch, steps),
        in_specs=[tok(d)] + [_resident(w.shape) for w in weights]
        + [tab, tab, _resident(xi.shape), _resident(ginv.shape), pl.BlockSpec(memory_space=pltpu.SMEM)],
        out_specs=[tok(w) for w in out_widths],
        out_shape=[jax.ShapeDtypeStruct((t, w), BF16) for w in out_widths],
        scratch_shapes=[pltpu.VMEM((n_heads, RET_QK_DIM, RET_V_DIM), F32)]
        + [pltpu.VMEM((TM_PROJ, w.shape[1]), BF16) for w in (w_qr, w_kr, w_vr, w_gr)],
        compiler_params=pltpu.CompilerParams(
            dimension_semantics=("parallel", "arbitrary"), vmem_limit_bytes=VMEM_LIMIT_BYTES),
        name="proj_retention",
    )(x2d, *weights, cos, sin, xi, ginv, gchunk)


def _attn_kernel(sink_ref, qkv_ref, kvp_ref, kg_ref, bd_ref, indt_ref, o_ref):
    n = pl.program_id(1)
    nsub = TS_ATTN // WINDOW
    d_q = o_ref.shape[1]
    n_pairs = d_q // LANES
    pairs_per_group = GROUP // 2
    n_groups = n_pairs // pairs_per_group

    kv_all = jnp.concatenate([kvp_ref[...], qkv_ref[:, d_q:]], axis=0).astype(F32)
    k_all = kv_all[:, :LANES]
    v_all = kv_all[:, LANES:]
    k_ms = jnp.dot((k_all * k_all).astype(BF16), bd_ref[...], preferred_element_type=F32) * (1.0 / HEAD_DIM)
    k_all = k_all * lax.rsqrt(k_ms + EPS) * kg_ref[...]
    lane = lax.broadcasted_iota(jnp.int32, k_all.shape, 1)
    lo = lane < HEAD_DIM
    k_sw = pltpu.roll(k_all, HEAD_DIM, axis=1)
    zero = jnp.zeros_like(k_all)
    k_top = [jnp.where(lo, k_all, zero).astype(BF16), jnp.where(lo, k_sw, zero).astype(BF16)]
    k_bot = [jnp.where(lo, zero, k_sw).astype(BF16), jnp.where(lo, zero, k_all).astype(BF16)]
    v_t = v_all.T
    ones_rows = jnp.ones((SUM_ROWS, v_t.shape[1]), F32)
    vt_ext = [jnp.concatenate([v_t[g * HEAD_DIM:(g + 1) * HEAD_DIM], ones_rows], axis=0).astype(BF16)
              for g in range(n_groups)]

    key = lax.broadcasted_iota(jnp.int32, (WINDOW, WINDOW), 0)
    qry = lax.broadcasted_iota(jnp.int32, (WINDOW, WINDOW), 1)
    upper = key > qry
    upper_bf = upper.astype(F32).astype(BF16)
    prev_pen = jnp.where(n == 0, NEG_BIG, 0.0).astype(F32)

    qnorm = {}

    def scores(j, g):
        q = qkv_ref[j * WINDOW:(j + 1) * WINDOW, :d_q]
        if j not in qnorm:
            ssq_t = lax.dot_general(indt_ref[...], q * q, (((1,), (1,)), ((), ())), preferred_element_type=F32)
            qnorm[j] = lax.rsqrt(ssq_t * (1.0 / HEAD_DIM) + EPS)
        p0 = g * pairs_per_group
        r0 = j * WINDOW
        qs = jnp.concatenate(
            [q[:, (p0 + pp) * LANES:(p0 + pp + 1) * LANES] for pp in range(pairs_per_group)], axis=0)
        kb = jnp.concatenate([k_top[g][r0:r0 + 2 * WINDOW], k_bot[g][r0:r0 + 2 * WINDOW]], axis=0)
        s_t = lax.dot_general(kb, qs, (((1,), (1,)), ((), ())), preferred_element_type=F32)
        return s_t, qnorm[j]

    def softmax_t(j, g, s_t, r_t):
        p0 = g * pairs_per_group
        prob_cols, sink_cols = [], []
        for pp in range(pairs_per_group):
            cs = slice(pp * WINDOW, (pp + 1) * WINDOW)
            for e in range(2):
                sp = s_t[e * 2 * WINDOW:e * 2 * WINDOW + WINDOW, cs]
                sc = s_t[e * 2 * WINDOW + WINDOW:(e + 1) * 2 * WINDOW, cs]
                if j == 0:
                    sp = sp + prev_pen
                comb = jnp.where(upper, sp, sc)
                hd = 2 * (p0 + pp) + e
                sink = sink_ref[hd]
                r_row = r_t[hd:hd + 1, :]
                m = jnp.maximum(jnp.max(comb, axis=0, keepdims=True) * r_row, sink)
                p = jnp.exp2(comb * r_row - m).astype(BF16)
                p_prev = p * upper_bf
                prob_cols.append(jnp.concatenate([p_prev, p - p_prev], axis=0))
                sink_cols.append(jnp.exp2(sink - m))
        return jnp.concatenate(prob_cols, axis=1), jnp.concatenate(sink_cols, axis=1)

    def outputs(j, g, p_t, sink_t):
        p0 = g * pairs_per_group
        r0 = j * WINDOW
        o_t = jnp.dot(vt_ext[g][:, r0:r0 + 2 * WINDOW], p_t, preferred_element_type=F32)
        inv = 1.0 / (o_t[HEAD_DIM:HEAD_DIM + 1, :] + sink_t)
        on = o_t[:HEAD_DIM, :] * inv
        for pp in range(pairs_per_group):
            pair = jnp.concatenate([on[:, (2 * pp) * WINDOW:(2 * pp + 1) * WINDOW],
                                    on[:, (2 * pp + 1) * WINDOW:(2 * pp + 2) * WINDOW]], axis=0)
            cols = slice((p0 + pp) * LANES, (p0 + pp + 1) * LANES)
            o_ref[j * WINDOW:(j + 1) * WINDOW, cols] = pair.T.astype(o_ref.dtype)

    units = [(j, g) for j in range(nsub) for g in range(n_groups)]
    pending = [scores(*u) for u in units[:SCORE_LOOKAHEAD]]
    probs = None
    for idx, unit in enumerate(units):
        if idx + SCORE_LOOKAHEAD < len(units):
            pending.append(scores(*units[idx + SCORE_LOOKAHEAD]))
        nxt = softmax_t(*unit, *pending.pop(0))
        if probs is not None:
            outputs(*units[idx - 1], *probs)
        probs = nxt
    outputs(*units[-1], *probs)


def _attention(qkv, sinks, k_gain_row, batch, seq, d_q):
    t = qkv.shape[0]
    assert seq % TS_ATTN == 0 and TS_ATTN % WINDOW == 0 and qkv.shape[1] == d_q + 2 * LANES
    steps = seq // TS_ATTN
    sub = TS_ATTN // WINDOW
    kv_blk = d_q // (2 * LANES)
    ii = np.arange(LANES)
    dd = np.arange(d_q)
    hh = np.arange(d_q // HEAD_DIM)
    bd = jnp.asarray((ii[:, None] // HEAD_DIM) == (ii[None, :] // HEAD_DIM), BF16)
    ind_t = jnp.asarray(hh[:, None] == (dd[None, :] // HEAD_DIM), BF16)
    return pl.pallas_call(
        _attn_kernel,
        grid=(batch, steps),
        in_specs=[
            pl.BlockSpec(memory_space=pltpu.SMEM),
            pl.BlockSpec((TS_ATTN, d_q + 2 * LANES), lambda b, n: (b * steps + n, 0)),
            pl.BlockSpec((WINDOW, 2 * LANES),
                         lambda b, n: ((b * steps + n) * sub - jnp.minimum(n, 1), kv_blk)),
            _resident((1, LANES)),
            _resident((LANES, LANES)),
            _resident(ind_t.shape),
        ],
        out_specs=pl.BlockSpec((TS_ATTN, d_q), lambda b, n: (b * steps + n, 0)),
        out_shape=jax.ShapeDtypeStruct((t, d_q), BF16),
        compiler_params=pltpu.CompilerParams(
            dimension_semantics=("parallel", "parallel"), vmem_limit_bytes=VMEM_LIMIT_BYTES),
        name="swa_attention",
    )(sinks, qkv, qkv, k_gain_row, bd, ind_t)


def _out_kernel(x_ref, attn_ref, ret_ref, zz_ref, wba_ref, wbr_ref, wo_ref,
                wg_ref, wu_ref, wd_ref, o_ref):
    d = x_ref.shape[1]
    ba = jnp.dot(attn_ref[...], wba_ref[...], preferred_element_type=F32)
    br = jnp.dot(ret_ref[...], wbr_ref[...], preferred_element_type=F32)
    merged = (jax.nn.sigmoid(zz_ref[:, :d].astype(F32)) * ba
              + jax.nn.sigmoid(zz_ref[:, d:].astype(F32)) * br)
    x1 = x_ref[...] + jnp.dot(merged.astype(BF16), wo_ref[...], preferred_element_type=F32)
    r = lax.rsqrt(jnp.mean(x1 * x1, axis=-1, keepdims=True) + EPS)
    x1b = x1.astype(BF16)
    h = (x1 * r).astype(BF16)
    acc = x1
    d_ff = wg_ref.shape[1]
    for c in range(0, d_ff, FF_CHUNK):
        if c < EARLY_CHUNKS * FF_CHUNK:
            gate = jnp.dot(x1b, wg_ref[:, c:c + FF_CHUNK], preferred_element_type=F32) * r
            up = jnp.dot(x1b, wu_ref[:, c:c + FF_CHUNK], preferred_element_type=F32) * r
        else:
            gate = jnp.dot(h, wg_ref[:, c:c + FF_CHUNK], preferred_element_type=F32)
            up = jnp.dot(h, wu_ref[:, c:c + FF_CHUNK], preferred_element_type=F32)
        act = (gate * jax.nn.sigmoid(gate) * up).astype(BF16)
        acc = acc + jnp.dot(act, wd_ref[c:c + FF_CHUNK, :], preferred_element_type=F32)
    o_ref[...] = acc


def _merge_ffn(x2d, attn, ret, zz, wba, wbr, wo, wg, wu, wd):
    t, d = x2d.shape
    assert t % TM_OUT == 0 and wg.shape[1] % FF_CHUNK == 0
    tok = lambda w: pl.BlockSpec((TM_OUT, w), lambda i: (i, 0))
    return pl.pallas_call(
        _out_kernel,
        grid=(t // TM_OUT,),
        in_specs=[tok(d), tok(attn.shape[1]), tok(ret.shape[1]), tok(zz.shape[1]),
                  _resident(wba.shape), _resident(wbr.shape), _resident(wo.shape),
                  _resident(wg.shape), _resident(wu.shape),
                  _resident(wd.shape)],
        out_specs=tok(d),
        out_shape=jax.ShapeDtypeStruct((t, d), F32),
        compiler_params=pltpu.CompilerParams(
            dimension_semantics=("parallel",), vmem_limit_bytes=VMEM_LIMIT_BYTES),
        name="merge_ffn",
    )(x2d, attn, ret, zz, wba, wbr, wo, wg, wu, wd)


def _deinterleave_heads(w, n_heads, head_dim):
    d = w.shape[0]
    src = np.concatenate([np.arange(0, head_dim, 2), np.arange(1, head_dim, 2)])
    perm = jnp.asarray(np.arange(head_dim)[:, None] == src[None, :], w.dtype)
    out = jnp.einsum('dhk,kj->dhj', w.reshape(d, n_heads, head_dim), perm, preferred_element_type=w.dtype)
    return out.reshape(d, n_heads * head_dim)


def kernel(x, norm_mix_gain, w_in, q_norm_gain, k_norm_gain, attn_sinks, w_branch_attn, w_branch_ret,
           w_out, norm_ffn_gain, w_ffn_gate, w_ffn_up, w_ffn_down):
    batch, seq, d = x.shape
    depth = w_in.shape[0]
    att_q = w_branch_attn.shape[1]
    ret_v = w_branch_ret.shape[1]
    n_q_heads = att_q // HEAD_DIM
    att_kv = (n_q_heads // GROUP) * HEAD_DIM
    ret_heads = ret_v // RET_V_DIM
    ret_qk = ret_heads * RET_QK_DIM
    splits = np.cumsum([att_q, att_kv, att_kv, ret_qk, ret_qk, ret_v, ret_v, d])
    assert att_kv == LANES and w_in.shape[2] == splits[-1] + d

    x2d = x.reshape(batch * seq, d)
    for l in range(depth):
        w_in_b = (norm_mix_gain[l][:, None] * w_in[l]).astype(BF16)
        w_attn = w_in_b[:, :splits[2]]
        _, wq_r, wk_r, wv_r, wg_r, w_z = jnp.split(w_in_b, splits[2:7], axis=1)
        qkv_a, zz, ret = _proj_retention(
            x2d, _deinterleave_heads(wq_r, ret_heads, RET_QK_DIM), _deinterleave_heads(wk_r, ret_heads, RET_QK_DIM),
            wv_r, wg_r, w_attn, w_z, batch, seq)

        k_gain_row = jnp.tile(k_norm_gain[l] * q_norm_gain[l] * (HEAD_DIM ** -0.5 * LOG2E),
                              LANES // HEAD_DIM)[None, :]
        attn = _attention(qkv_a, attn_sinks[l] * LOG2E, k_gain_row, batch, seq, att_q)

        x2d = _merge_ffn(
            x2d, attn, ret, zz,
            w_branch_attn[l].astype(BF16), w_branch_ret[l].astype(BF16), w_out[l].astype(BF16),
            (norm_ffn_gain[l][:, None] * w_ffn_gate[l]).astype(BF16),
            (norm_ffn_gain[l][:, None] * w_ffn_up[l]).astype(BF16), w_ffn_down[l].astype(BF16))
    return x2d.reshape(batch, seq, d)
```
